```python
import math
import jax, jax.numpy as jnp
from jax import lax
import numpy as np

D_MODEL = 1024
BATCH = 8
SEQ = 4096
DEPTH = 1

CHUNK = 64
Q_BLOCK = 128
EPS = 1e-6
DA_HEADS = 8
DA_HEAD_DIM = 64
DA_WIDTH = DA_HEADS * 2 * DA_HEAD_DIM
ROPE_THETA = 10000.0
ML_HEADS = 4
ML_WIDTH = D_MODEL
ML_HEAD_DIM = ML_WIDTH // ML_HEADS
CONV_WIDTH = 4
D_FF = 2816
FFN_RES = 0.5
N_MOD = 9

kernel_name = "hybrid_diffattn_mlstm_macaron_adaln"


def rmsnorm(x, g):
    x32 = x.astype(jnp.float32)
    y = x32 * lax.rsqrt(jnp.mean(x32 * x32, axis=-1, keepdims=True) + EPS)
    return (y * g.astype(jnp.float32)).astype(x.dtype)


def modulate(xn, shift, scale):
    return xn * (1.0 + scale[:, None, :]) + shift[:, None, :]


def swiglu(u, w_gate, w_up, w_down):
    return (jax.nn.silu(u @ w_gate) * (u @ w_up)) @ w_down


def rope(t):
    S, Dh = t.shape[1], t.shape[-1]
    inv = ROPE_THETA ** (-jnp.arange(0, Dh, 2, dtype=jnp.float32) / Dh)
    ang = jnp.arange(S, dtype=jnp.float32)[:, None] * inv[None, :]
    cos = jnp.cos(ang)[None, :, None, None, :].astype(t.dtype)
    sin = jnp.sin(ang)[None, :, None, None, :].astype(t.dtype)
    t1, t2 = jnp.split(t, 2, axis=-1)
    return jnp.concatenate([t1 * cos - t2 * sin, t1 * sin + t2 * cos], axis=-1)


def diff_attention(q, k, v, lam, g_subln, lambda_init):
    B, S, H, _, Dh = q.shape
    scale = Dh ** -0.5
    q = rope(q)
    k = rope(k)
    nb = S // Q_BLOCK
    qb = jnp.moveaxis(q.reshape(B, nb, Q_BLOCK, H, 2, Dh), 1, 0)
    key_chunk = jnp.arange(S) // CHUNK

    def block(args):
        qi, bi = args
        s = jnp.einsum('bqhcd,bkhcd->bchqk', qi, k,
                       preferred_element_type=jnp.float32) * scale
        q_chunk = (bi * Q_BLOCK + jnp.arange(Q_BLOCK)) // CHUNK
        mask = key_chunk[None, :] <= q_chunk[:, None]
        s = jnp.where(mask, s, -jnp.inf)
        p = jax.nn.softmax(s, axis=-1)
        a = p[:, 0] - lam * p[:, 1]
        return jnp.einsum('bhqk,bkhd->bqhd', a.astype(v.dtype), v)

    o = lax.map(block, (qb, jnp.arange(nb)))
    o = jnp.moveaxis(o, 0, 1).reshape(B, S, H, 2 * Dh)
    o = rmsnorm(o, g_subln) * (1.0 - lambda_init)
    return o.reshape(B, S, H * 2 * Dh)


def causal_conv(x, w, b):
    K, C = w.shape
    y = lax.conv_general_dilated(x, w[:, None, :].astype(x.dtype), window_strides=(1,),
                                 padding=[(K - 1, 0)],
                                 dimension_numbers=('NWC', 'WIO', 'NWC'),
                                 feature_group_count=C)
    return y + b


def mlstm_chunkwise(q, k, v, i_pre, f_pre):
    B, H, S, D = q.shape
    L = CHUNK
    nc = S // L
    out_dtype = v.dtype
    q = q.astype(jnp.float32)
    k = k.astype(jnp.float32) * (D ** -0.5)
    v = v.astype(jnp.float32)
    ig = i_pre.astype(jnp.float32)
    logf = jax.nn.log_sigmoid(f_pre.astype(jnp.float32))

    def to_chunks(t):
        return jnp.moveaxis(t.reshape(B, H, nc, L, *t.shape[3:]), 2, 0)

    tril = jnp.tril(jnp.ones((L, L), dtype=bool))

    def body(carry, inp):
        C, n, m = carry
        qc, kc, vc, ic, fc = inp
        b = jnp.cumsum(fc, axis=-1)
        dmat = jnp.where(tril, b[..., :, None] - b[..., None, :] + ic[..., None, :], -jnp.inf)
        inter = b + m[..., None]
        m_t = jnp.maximum(jnp.max(dmat, axis=-1), inter)
        w = jnp.exp(dmat - m_t[..., None])
        sc = jnp.einsum('bhtd,bhsd->bhts', qc, kc) * w
        w_inter = jnp.exp(inter - m_t)
        num = jnp.einsum('bhts,bhsd->bhtd', sc, vc) + w_inter[..., None] * jnp.einsum('bhtd,bhde->bhte', qc, C)
        den = jnp.sum(sc, axis=-1) + w_inter * jnp.einsum('bhtd,bhd->bht', qc, n)
        h = num / jnp.maximum(jnp.abs(den), jnp.exp(-m_t))[..., None]
        bL = b[..., -1]
        g = bL[..., None] - b + ic
        m_new = jnp.maximum(bL + m, jnp.max(g, axis=-1))
        decay = jnp.exp(bL + m - m_new)
        wk = jnp.exp(g - m_new[..., None])[..., None] * kc
        C_new = decay[..., None, None] * C + jnp.einsum('bhsd,bhse->bhde', wk, vc)
        n_new = decay[..., None] * n + jnp.sum(wk, axis=-2)
        return (C_new, n_new, m_new), h

    init = (jnp.zeros((B, H, D, D), jnp.float32), jnp.zeros((B, H, D), jnp.float32),
            jnp.zeros((B, H), jnp.float32))
    _, hs = lax.scan(body, init, (to_chunks(q), to_chunks(k), to_chunks(v),
                                  to_chunks(ig), to_chunks(logf)))
    hs = jnp.moveaxis(hs, 0, 2).reshape(B, H, S, D)
    return hs.astype(out_dtype)


def mlstm_branch(xm, o_pre, conv_w, conv_b, w_mq, w_mk, w_mv, w_if, b_if, ml_skip, g_mlnorm):
    B, S, _ = xm.shape
    H, Dh = ML_HEADS, ML_HEAD_DIM
    xc = jax.nn.silu(causal_conv(xm, conv_w, conv_b))
    xch = xc.reshape(B, S, H, Dh)
    xmh = xm.reshape(B, S, H, Dh)
    q = jnp.einsum('bshd,hde->bhse', xch, w_mq)
    k = jnp.einsum('bshd,hde->bhse', xch, w_mk)
    v = jnp.einsum('bshd,hde->bhse', xmh, w_mv)
    gates = (jnp.einsum('bhse,heg->bgs', q, w_if[0]) + jnp.einsum('bhse,heg->bgs', k, w_if[1])
             + jnp.einsum('bhse,heg->bgs', v, w_if[2]) + b_if[None, :, None])
    hcell = mlstm_chunkwise(q, k, v, gates[:, :H], gates[:, H:])
    hcell = jnp.swapaxes(hcell, 1, 2)
    hn = rmsnorm(hcell, g_mlnorm.reshape(H, Dh))
    y = (hn + ml_skip.reshape(H, Dh) * xch) * jax.nn.sigmoid(o_pre).reshape(B, S, H, Dh)
    return y.reshape(B, S, H * Dh)


def setup_inputs(seed: int = 0) -> dict:
    key = jax.random.key(seed)
    ks = jax.random.split(key, 40)
    L = DEPTH

    def nrm(k, shape, scale):
        return jax.random.normal(k, shape, jnp.float32) * scale

    def gain(k, shape):
        return 1.0 + 0.05 * jax.random.normal(k, shape, jnp.float32)

    in_cols = 3 * DA_WIDTH + 2 * ML_WIDTH + 2 * D_MODEL
    f_bias = jnp.linspace(3.0, 6.0, ML_HEADS, dtype=jnp.float32)
    b_if = jnp.concatenate([nrm(ks[30], (L, ML_HEADS), 0.1),
                            f_bias[None, :] + nrm(ks[31], (L, ML_HEADS), 0.1)], axis=-1)
    return {
        "x": nrm(ks[0], (BATCH, SEQ, D_MODEL), 1.0),
        "c": nrm(ks[1], (BATCH, D_MODEL), 1.0),
        "w_ada": nrm(ks[2], (L, D_MODEL, N_MOD * D_MODEL), D_MODEL ** -0.5),
        "b_ada": nrm(ks[3], (L, N_MOD * D_MODEL), 0.02),
        "g_ff1": gain(ks[4], (L, D_MODEL)),
        "w1_gate": nrm(ks[5], (L, D_MODEL, D_FF), D_MODEL ** -0.5),
        "w1_up": nrm(ks[6], (L, D_MODEL, D_FF), D_MODEL ** -0.5),
        "w1_down": nrm(ks[7], (L, D_FF, D_MODEL), D_FF ** -0.5),
        "g_mix": gain(ks[8], (L, D_MODEL)),
        "w_in": nrm(ks[9], (L, D_MODEL, in_cols), D_MODEL ** -0.5),
        "lambda_q1": nrm(ks[10], (L, DA_HEAD_DIM), 0.1),
        "lambda_k1": nrm(ks[11], (L, DA_HEAD_DIM), 0.1),
        "lambda_q2": nrm(ks[12], (L, DA_HEAD_DIM), 0.1),
        "lambda_k2": nrm(ks[13], (L, DA_HEAD_DIM), 0.1),
        "g_subln": gain(ks[14], (L, 2 * DA_HEAD_DIM)),
        "conv_w": nrm(ks[15], (L, CONV_WIDTH, ML_WIDTH), CONV_WIDTH ** -0.5),
        "conv_b": nrm(ks[16], (L, ML_WIDTH), 0.02),
        "w_mq": nrm(ks[17], (L, ML_HEADS, ML_HEAD_DIM, ML_HEAD_DIM), ML_HEAD_DIM ** -0.5),
        "w_mk": nrm(ks[18], (L, ML_HEADS, ML_HEAD_DIM, ML_HEAD_DIM), ML_HEAD_DIM ** -0.5),
        "w_mv": nrm(ks[19], (L, ML_HEADS, ML_HEAD_DIM, ML_HEAD_DIM), ML_HEAD_DIM ** -0.5),
        "w_if": nrm(ks[20], (L, 3, ML_HEADS, ML_HEAD_DIM, 2 * ML_HEADS), (3 * ML_WIDTH) ** -0.5),
        "b_if": b_if,
        "ml_skip": gain(ks[21], (L, ML_WIDTH)),
        "g_mlnorm": gain(ks[22], (L, ML_WIDTH)),
        "w_proj_a": nrm(ks[23], (L, DA_WIDTH, D_MODEL), DA_WIDTH ** -0.5),
        "w_proj_b": nrm(ks[24], (L, ML_WIDTH, D_MODEL), ML_WIDTH ** -0.5),
        "w_out": nrm(ks[25], (L, D_MODEL, D_MODEL), D_MODEL ** -0.5),
        "g_ff2": gain(ks[26], (L, D_MODEL)),
        "w2_gate": nrm(ks[27], (L, D_MODEL, D_FF), D_MODEL ** -0.5),
        "w2_up": nrm(ks[28], (L, D_MODEL, D_FF), D_MODEL ** -0.5),
        "w2_down": nrm(ks[29], (L, D_FF, D_MODEL), D_FF ** -0.5),
        "g_final": gain(ks[32], (D_MODEL,)),
    }


def reference(x, c, w_ada, b_ada, g_ff1, w1_gate, w1_up, w1_down, g_mix, w_in,
              lambda_q1, lambda_k1, lambda_q2, lambda_k2, g_subln, conv_w, conv_b,
              w_mq, w_mk, w_mv, w_if, b_if, ml_skip, g_mlnorm, w_proj_a, w_proj_b,
              w_out, g_ff2, w2_gate, w2_up, w2_down, g_final):
    B, S, D = x.shape
    split_idx = list(np.cumsum([DA_WIDTH, DA_WIDTH, DA_WIDTH, ML_WIDTH, ML_WIDTH, D_MODEL])[:])
    h = x
    for l in range(DEPTH):
        lambda_init = 0.8 - 0.6 * math.exp(-0.3 * l)
        mods = jnp.split(jax.nn.silu(c) @ w_ada[l] + b_ada[l], N_MOD, axis=-1)
        sh1, sc1, gt1, sh2, sc2, gt2, sh3, sc3, gt3 = mods

        u = modulate(rmsnorm(h, g_ff1[l]), sh1, sc1)
        h = h + FFN_RES * gt1[:, None, :] * swiglu(u, w1_gate[l], w1_up[l], w1_down[l])

        u = modulate(rmsnorm(h, g_mix[l]), sh2, sc2)
        proj = u @ w_in[l]
        qa, ka, va, xm, o_pre, ga, gb = jnp.split(proj, split_idx, axis=-1)

        lam = (jnp.exp(jnp.sum(lambda_q1[l].astype(jnp.float32) * lambda_k1[l].astype(jnp.float32)))
               - jnp.exp(jnp.sum(lambda_q2[l].astype(jnp.float32) * lambda_k2[l].astype(jnp.float32)))
               + lambda_init)
        ya = diff_attention(qa.reshape(B, S, DA_HEADS, 2, DA_HEAD_DIM),
                            ka.reshape(B, S, DA_HEADS, 2, DA_HEAD_DIM),
                            va.reshape(B, S, DA_HEADS, 2 * DA_HEAD_DIM),
                            lam, g_subln[l], lambda_init)
        yb = mlstm_branch(xm, o_pre, conv_w[l], conv_b[l], w_mq[l], w_mk[l], w_mv[l],
                          w_if[l], b_if[l], ml_skip[l], g_mlnorm[l])
        merged = jax.nn.sigmoid(ga) * (ya @ w_proj_a[l]) + jax.nn.sigmoid(gb) * (yb @ w_proj_b[l])
        h = h + gt2[:, None, :] * (merged @ w_out[l])

        u = modulate(rmsnorm(h, g_ff2[l]), sh3, sc3)
        h = h + FFN_RES * gt3[:, None, :] * swiglu(u, w2_gate[l], w2_up[l], w2_down[l])
    return rmsnorm(h, g_final)
```

```python
import functools
import math

import jax
import jax.numpy as jnp
from jax import lax
from jax.experimental import pallas as pl
from jax.experimental.pallas import tpu as pltpu

D_MODEL = 1024
CHUNK = 64
EPS = 1e-6
DA_HEADS = 8
DA_HEAD_DIM = 64
ROPE_THETA = 10000.0
ML_HEADS = 4
ML_HEAD_DIM = 256
CONV_WIDTH = 4
D_FF = 2816
FFN_RES = 0.5
N_MOD = 9
LAMBDA_INIT = 0.8 - 0.6 * math.exp(-0.3 * 0)

V7X_LANES = 128
V7X_SUBLANES = 8
V7X_VMEM_LIMIT_BYTES = 56 * 1024 * 1024

BF16 = jnp.bfloat16
F32 = jnp.float32


def _resident(shape):
    nd = len(shape)
    return pl.BlockSpec(shape, lambda *_: (0,) * nd, pipeline_mode=pl.Buffered(1))


def _params(*sem):
    return pltpu.CompilerParams(dimension_semantics=sem,
                                vmem_limit_bytes=V7X_VMEM_LIMIT_BYTES)


def _mod_spec(k):
    return pl.BlockSpec((None, None, 1, D_MODEL), lambda b, i: (k, b, 0, 0))


def _rms_mod(x, g, shift, scale):
    ms = jnp.mean(x * x, axis=-1, keepdims=True)
    xn = x * lax.rsqrt(ms + EPS) * g
    return xn * (1.0 + scale) + shift


def _sigmoid(x):
    return 1.0 / (1.0 + jnp.exp(-x))


def _ada_kernel(c_ref, w_ref, b_ref, o_ref):
    c = c_ref[...]
    a = c * _sigmoid(c)
    o_ref[...] = jnp.dot(a, w_ref[...], preferred_element_type=F32) + b_ref[...]


def _ada(c, w_ada, b_ada):
    B = c.shape[0]
    return pl.pallas_call(
        _ada_kernel,
        out_shape=jax.ShapeDtypeStruct((N_MOD, B, D_MODEL), F32),
        grid=(N_MOD,),
        in_specs=[
            pl.BlockSpec((B, D_MODEL), lambda j: (0, 0)),
            pl.BlockSpec((D_MODEL, D_MODEL), lambda j: (0, j)),
            pl.BlockSpec((None, 1, D_MODEL), lambda j: (j, 0, 0)),
        ],
        out_specs=pl.BlockSpec((None, B, D_MODEL), lambda j: (j, 0, 0)),
        compiler_params=_params("arbitrary"),
        name="ada",
    )(c, w_ada, b_ada.reshape(N_MOD, 1, D_MODEL))


def _ffn_kernel(h_ref, sh_ref, sc_ref, gt_ref, g_ref, wg_ref, wu_ref, wd_ref,
                gfin_ref, o_ref, *, final_norm):
    x = h_ref[...]
    u = _rms_mod(x, g_ref[...], sh_ref[...], sc_ref[...]).astype(BF16)
    gate = jnp.dot(u, wg_ref[...], preferred_element_type=F32)
    up = jnp.dot(u, wu_ref[...], preferred_element_type=F32)
    a = (gate * _sigmoid(gate) * up).astype(BF16)
    y = jnp.dot(a, wd_ref[...], preferred_element_type=F32)
    out = x + FFN_RES * gt_ref[...] * y
    if final_norm:
        ms = jnp.mean(out * out, axis=-1, keepdims=True)
        out = out * lax.rsqrt(ms + EPS) * gfin_ref[...]
    o_ref[...] = out


def _ffn(h, mods, k0, g, wg, wu, wd, g_final, *, final_norm, tm=512):
    B, S, D = h.shape
    row = pl.BlockSpec((None, tm, D), lambda b, i: (b, i, 0))
    return pl.pallas_call(
        functools.partial(_ffn_kernel, final_norm=final_norm),
        out_shape=jax.ShapeDtypeStruct((B, S, D), F32),
        grid=(B, S // tm),
        in_specs=[row, _mod_spec(k0), _mod_spec(k0 + 1), _mod_spec(k0 + 2),
                  _resident((1, D)), _resident((D, D_FF)), _resident((D, D_FF)),
                  _resident((D_FF, D)), _resident((1, D))],
        out_specs=row,
        compiler_params=_params("parallel", "parallel"),
        name="ffn_final" if final_norm else "ffn",
    )(h, mods, mods, mods, g.reshape(1, D), wg, wu, wd, g_final.reshape(1, D))


def _rope(t, cos, sin_signed, first_half):
    outs = []
    for gidx in range(t.shape[1] // V7X_LANES):
        x = t[:, gidx * V7X_LANES:(gidx + 1) * V7X_LANES]
        partner = jnp.where(first_half,
                            pltpu.roll(x, V7X_LANES - DA_HEAD_DIM // 2, 1),
                            pltpu.roll(x, DA_HEAD_DIM // 2, 1))
        outs.append(x * cos + partner * sin_signed)
    return outs


def _proj_kernel(h_ref, sh_ref, sc_ref, g_ref, w_ref, cos_ref, sin_ref,
                 q_ref, k1_ref, k2_ref, v_ref, xm_ref, op_ref, ga_ref, gb_ref):
    x = h_ref[...]
    u = _rms_mod(x, g_ref[...], sh_ref[...], sc_ref[...]).astype(BF16)

    def section(s):
        return jnp.dot(u, w_ref[:, s * D_MODEL:(s + 1) * D_MODEL],
                       preferred_element_type=F32)

    cos = cos_ref[...]
    sin = sin_ref[...]
    lane = lax.broadcasted_iota(jnp.int32, (x.shape[0], V7X_LANES), 1)
    first_half = (lane % DA_HEAD_DIM) < (DA_HEAD_DIM // 2)
    comp0 = lane < DA_HEAD_DIM

    q_scale = DA_HEAD_DIM ** -0.5
    for gidx, t in enumerate(_rope(section(0), cos, sin, first_half)):
        q_ref[:, gidx * V7X_LANES:(gidx + 1) * V7X_LANES] = (t * q_scale).astype(BF16)
    for gidx, t in enumerate(_rope(section(1), cos, sin, first_half)):
        sl = slice(gidx * V7X_LANES, (gidx + 1) * V7X_LANES)
        k1_ref[:, sl] = jnp.where(comp0, t, 0.0).astype(BF16)
        k2_ref[:, sl] = jnp.where(comp0, 0.0, t).astype(BF16)
    v_ref[...] = section(2).astype(BF16)
    xm_ref[...] = section(3)
    op_ref[...] = section(4)
    ga_ref[...] = section(5)
    gb_ref[...] = section(6)


def _proj(h, mods, g, w_in, cos, sin, *, tm=512):
    B, S, D = h.shape
    row = pl.BlockSpec((None, tm, D), lambda b, i: (b, i, 0))
    tab = pl.BlockSpec((tm, V7X_LANES), lambda b, i: (i, 0))
    bf = jax.ShapeDtypeStruct((B, S, D), BF16)
    f32 = jax.ShapeDtypeStruct((B, S, D), F32)
    return pl.pallas_call(
        _proj_kernel,
        out_shape=(bf, bf, bf, bf, f32, f32, f32, f32),
        grid=(B, S // tm),
        in_specs=[row, _mod_spec(3), _mod_spec(4), _resident((1, D)),
                  _resident((D, 7 * D)), tab, tab],
        out_specs=(row,) * 8,
        compiler_params=_params("parallel", "parallel"),
        name="mixer_proj",
    )(h, mods, mods, g.reshape(1, D), w_in, cos, sin)


def _attn_kernel(lq1_ref, lk1_ref, lq2_ref, lk2_ref, gsub_ref,
                 q_ref, k1_ref, k2_ref, v_ref, o_ref, *, tq):
    qi = pl.program_id(2)
    q = q_ref[...]
    nt = (((1,), (1,)), ((), ()))

    def step(j, carry, masked):
        start = pl.multiple_of(j * tq, tq)
        vblk = v_ref[pl.ds(start, tq), :]
        new = []
        for c, k_ref in enumerate((k1_ref, k2_ref)):
            m, l, acc = carry[3 * c:3 * c + 3]
            s = lax.dot_general(q, k_ref[pl.ds(start, tq), :], nt,
                                preferred_element_type=F32)
            if masked:
                r = lax.broadcasted_iota(jnp.int32, (tq, tq), 0) // CHUNK
                cc = lax.broadcasted_iota(jnp.int32, (tq, tq), 1) // CHUNK
                s = jnp.where(cc <= r, s, -jnp.inf)
            m_new = jnp.maximum(m, jnp.max(s, axis=-1, keepdims=True))
            alpha = jnp.exp(m - m_new)
            p = jnp.exp(s - m_new)
            l = alpha * l + jnp.sum(p, axis=-1, keepdims=True)
            acc = alpha * acc + jnp.dot(p.astype(BF16), vblk,
                                        preferred_element_type=F32)
            new += [m_new, l, acc]
        return tuple(new)

    init = []
    for _ in range(2):
        init += [jnp.full((tq, 1), -jnp.inf, F32), jnp.zeros((tq, 1), F32),
                 jnp.zeros((tq, V7X_LANES), F32)]
    carry = lax.fori_loop(0, qi, lambda j, cr: step(j, cr, False), tuple(init))
    m1, l1, a1, m2, l2, a2 = step(qi, carry, True)

    lam = (jnp.exp(jnp.sum(lq1_ref[...] * lk1_ref[...], axis=-1, keepdims=True))
           - jnp.exp(jnp.sum(lq2_ref[...] * lk2_ref[...], axis=-1, keepdims=True))
           + LAMBDA_INIT)
    o = a1 / l1 - lam * (a2 / l2)
    ms = jnp.mean(o * o, axis=-1, keepdims=True)
    o = o * lax.rsqrt(ms + EPS) * gsub_ref[...] * (1.0 - LAMBDA_INIT)
    o_ref[...] = o.astype(o_ref.dtype)


def _attn(q, k1, k2, v, lq1, lk1, lq2, lk2, g_subln, *, tq=512):
    B, S, D = q.shape
    lam_spec = pl.BlockSpec((1, DA_HEAD_DIM), lambda b, h, i: (0, 0))
    qo = pl.BlockSpec((None, tq, V7X_LANES), lambda b, h, i: (b, i, h))
    kv = pl.BlockSpec((None, S, V7X_LANES), lambda b, h, i: (b, 0, h))
    return pl.pallas_call(
        functools.partial(_attn_kernel, tq=tq),
        out_shape=jax.ShapeDtypeStruct((B, S, D), BF16),
        grid=(B, DA_HEADS, S // tq),
        in_specs=[lam_spec] * 4 + [pl.BlockSpec((1, V7X_LANES), lambda b, h, i: (0, 0)),
                                   qo, kv, kv, kv],
        out_specs=qo,
        compiler_params=_params("parallel", "parallel", "arbitrary"),
        name="diff_attn",
    )(lq1, lk1, lq2, lk2, g_subln, q, k1, k2, v)


def _cumsum_rows(x):
    n = x.shape[0]
    row = lax.broadcasted_iota(jnp.int32, x.shape, 0)
    shift = 1
    while shift < n:
        x = x + jnp.where(row >= shift, pltpu.roll(x, shift, 0), 0.0)
        shift *= 2
    return x


def _lane(x, h):
    return x[:, h:h + 1]


def _mlstm_kernel(xm_ref, op_ref, cw_ref, cb_ref, wq_ref, wk_ref, wv_ref,
                  wif_ref, bif_ref, skip_ref, gn_ref, yb_ref,
                  xbuf, c_st, n_st, m_st, *, L):
    H, Dh = ML_HEADS, ML_HEAD_DIM
    halo = V7X_SUBLANES

    @pl.when(pl.program_id(1) == 0)
    def _():
        xbuf[0:halo, :] = jnp.zeros((halo, H * Dh), F32)
        c_st[...] = jnp.zeros_like(c_st)
        n_st[...] = jnp.zeros_like(n_st)
        m_st[...] = jnp.zeros_like(m_st)

    xm = xm_ref[...]
    xbuf[halo:halo + L, :] = xm
    conv = cb_ref[...] + cw_ref[CONV_WIDTH - 1:CONV_WIDTH, :] * xm
    for d in range(1, CONV_WIDTH):
        conv = conv + (cw_ref[CONV_WIDTH - 1 - d:CONV_WIDTH - d, :]
                       * xbuf[halo - d:halo - d + L, :])
    xbuf[0:halo, :] = xm[L - halo:L, :]
    xc = conv * _sigmoid(conv)

    qs, ks, vs = [], [], []
    gates = bif_ref[...]
    for h in range(H):
        sl = slice(h * Dh, (h + 1) * Dh)
        xc_h = xc[:, sl].astype(BF16)
        q = jnp.dot(xc_h, wq_ref[h], preferred_element_type=F32)
        k = jnp.dot(xc_h, wk_ref[h], preferred_element_type=F32)
        v = jnp.dot(xm[:, sl].astype(BF16), wv_ref[h], preferred_element_type=F32)
        for j, t in enumerate((q, k, v)):
            gates = gates + jnp.dot(t.astype(BF16), wif_ref[j, h],
                                    preferred_element_type=F32)
        qs.append(q)
        ks.append(k * (Dh ** -0.5))
        vs.append(v)

    ig = gates[:, :V7X_LANES]
    logf = jax.nn.log_sigmoid(gates[:, V7X_LANES:])
    b = _cumsum_rows(logf)
    a = ig - b
    a_t = a.T
    m_prev = m_st[...]
    inter = b + m_prev
    b_last = b[L - 1:L, :]
    m_new = jnp.maximum(b_last + m_prev, jnp.max(b_last + a, axis=0, keepdims=True))
    decay = jnp.exp(b_last + m_prev - m_new)
    wg = jnp.exp(b_last + a - m_new)
    m_st[...] = m_new

    tril = (lax.broadcasted_iota(jnp.int32, (L, L), 1)
            <= lax.broadcasted_iota(jnp.int32, (L, L), 0))
    nt = (((1,), (1,)), ((), ()))
    tn = (((0,), (0,)), ((), ()))
    for h in range(H):
        sl = slice(h * Dh, (h + 1) * Dh)
        q, k, v = qs[h], ks[h], vs[h]
        qb, kb, vb = q.astype(BF16), k.astype(BF16), v.astype(BF16)
        dmat = jnp.where(tril, _lane(b, h) + a_t[h:h + 1, :], -jnp.inf)
        m_t = jnp.maximum(jnp.max(dmat, axis=-1, keepdims=True), _lane(inter, h))
        w = jnp.exp(dmat - m_t)
        sc = lax.dot_general(qb, kb, nt, preferred_element_type=F32) * w
        w_inter = jnp.exp(_lane(inter, h) - m_t)
        c_prev = c_st[h]
        n_prev = n_st[h]
        num = (jnp.dot(sc.astype(BF16), vb, preferred_element_type=F32)
               + w_inter * jnp.dot(qb, c_prev.astype(BF16), preferred_element_type=F32))
        den = (jnp.sum(sc, axis=-1, keepdims=True)
               + w_inter * jnp.sum(q * n_prev, axis=-1, keepdims=True))
        hcell = num / jnp.maximum(jnp.abs(den), jnp.exp(-m_t))

        wk = _lane(wg, h) * k
        dec = _lane(decay, h)
        c_st[h] = dec * c_prev + lax.dot_general(wk.astype(BF16), vb, tn,
                                                 preferred_element_type=F32)
        n_st[h] = dec * n_prev + jnp.sum(wk, axis=0, keepdims=True)

        ms = jnp.mean(hcell * hcell, axis=-1, keepdims=True)
        hn = hcell * lax.rsqrt(ms + EPS) * gn_ref[:, sl]
        y = (hn + skip_ref[:, sl] * xc[:, sl]) * _sigmoid(op_ref[:, sl])
        yb_ref[:, sl] = y.astype(yb_ref.dtype)


def _mlstm(xm, o_pre, conv_w, conv_b, wq, wk, wv, wif, bif, ml_skip, g_mlnorm, *, L=256):
    B, S, D = xm.shape
    H, Dh = ML_HEADS, ML_HEAD_DIM
    row = pl.BlockSpec((None, L, D), lambda b, i: (b, i, 0))
    return pl.pallas_call(
        functools.partial(_mlstm_kernel, L=L),
        out_shape=jax.ShapeDtypeStruct((B, S, D), BF16),
        grid=(B, S // L),
        in_specs=[row, row, _resident((CONV_WIDTH, D)), _resident((1, D)),
                  _resident((H, Dh, Dh)), _resident((H, Dh, Dh)), _resident((H, Dh, Dh)),
                  _resident((3, H, Dh, 2 * V7X_LANES)), _resident((1, 2 * V7X_LANES)),
                  _resident((1, D)), _resident((1, D))],
        out_specs=row,
        scratch_shapes=[pltpu.VMEM((V7X_SUBLANES + L, D), F32),
                        pltpu.VMEM((H, Dh, Dh), F32),
                        pltpu.VMEM((H, 1, Dh), F32),
                        pltpu.VMEM((1, V7X_LANES), F32)],
        compiler_params=_params("parallel", "arbitrary"),
        name="mlstm",
    )(xm, o_pre, conv_w, conv_b.reshape(1, D), wq, wk, wv, wif, bif,
      ml_skip.reshape(1, D), g_mlnorm.reshape(1, D))


def _merge_kernel(h_ref, gt_ref, ya_ref, yb_ref, ga_ref, gb_ref,
                  wa_ref, wb_ref, wo_ref, o_ref):
    pa = jnp.dot(ya_ref[...], wa_ref[...], preferred_element_type=F32)
    pb = jnp.dot(yb_ref[...], wb_ref[...], preferred_element_type=F32)
    merged = _sigmoid(ga_ref[...]) * pa + _sigmoid(gb_ref[...]) * pb
    y = jnp.dot(merged.astype(BF16), wo_ref[...], preferred_element_type=F32)
    o_ref[...] = h_ref[...] + gt_ref[...] * y


def _merge(h, mods, ya, yb, ga, gb, wa, wb, wo, *, tm=512):
    B, S, D = h.shape
    row = pl.BlockSpec((None, tm, D), lambda b, i: (b, i, 0))
    return pl.pallas_call(
        _merge_kernel,
        out_shape=jax.ShapeDtypeStruct((B, S, D), F32),
        grid=(B, S // tm),
        in_specs=[row, _mod_spec(5), row, row, row, row,
                  _resident((D, D)), _resident((D, D)), _resident((D, D))],
        out_specs=row,
        compiler_params=_params("parallel", "parallel"),
        name="merge",
    )(h, mods, ya, yb, ga, gb, wa, wb, wo)


def _rope_tables(S):
    half = DA_HEAD_DIM // 2
    inv = ROPE_THETA ** (-jnp.arange(0, DA_HEAD_DIM, 2, dtype=F32) / DA_HEAD_DIM)
    ang = jnp.arange(S, dtype=F32)[:, None] * inv[None, :]
    cos, sin = jnp.cos(ang), jnp.sin(ang)
    reps = V7X_LANES // DA_HEAD_DIM
    cos_t = jnp.tile(jnp.concatenate([cos, cos], axis=-1), (1, reps))
    sin_t = jnp.tile(jnp.concatenate([-sin, sin], axis=-1), (1, reps))
    assert cos_t.shape == (S, V7X_LANES) and half * 2 == DA_HEAD_DIM
    return cos_t, sin_t


def _gate_weights(w_if, b_if):
    H = ML_HEADS
    pad = V7X_LANES - H
    wi = jnp.pad(w_if[..., :H], ((0, 0),) * 3 + ((0, pad),))
    wf = jnp.pad(w_if[..., H:], ((0, 0),) * 3 + ((0, pad),))
    bi = jnp.pad(b_if[:H], (0, pad))
    bf = jnp.pad(b_if[H:], (0, pad))
    return (jnp.concatenate([wi, wf], axis=-1).astype(BF16),
            jnp.concatenate([bi, bf]).reshape(1, 2 * V7X_LANES))


def kernel(x, c, w_ada, b_ada, g_ff1, w1_gate, w1_up, w1_down, g_mix, w_in,
           lambda_q1, lambda_k1, lambda_q2, lambda_k2, g_subln, conv_w, conv_b,
           w_mq, w_mk, w_mv, w_if, b_if, ml_skip, g_mlnorm, w_proj_a, w_proj_b,
           w_out, g_ff2, w2_gate, w2_up, w2_down, g_final):
    B, S, D = x.shape
    l = 0
    mods = _ada(c, w_ada[l], b_ada[l]).reshape(N_MOD, B, 1, D)
    cos_t, sin_t = _rope_tables(S)
    wif, bif = _gate_weights(w_if[l], b_if[l])

    h = _ffn(x, mods, 0, g_ff1[l], w1_gate[l].astype(BF16), w1_up[l].astype(BF16),
             w1_down[l].astype(BF16), g_final, final_norm=False)
    q, k1, k2, v, xm, o_pre, ga, gb = _proj(h, mods, g_mix[l], w_in[l].astype(BF16),
                                            cos_t, sin_t)
    ya = _attn(q, k1, k2, v, lambda_q1[l].reshape(1, -1), lambda_k1[l].reshape(1, -1),
               lambda_q2[l].reshape(1, -1), lambda_k2[l].reshape(1, -1),
               g_subln[l].reshape(1, -1))
    yb = _mlstm(xm, o_pre, conv_w[l], conv_b[l], w_mq[l].astype(BF16),
                w_mk[l].astype(BF16), w_mv[l].astype(BF16), wif, bif,
                ml_skip[l], g_mlnorm[l])
    h = _merge(h, mods, ya, yb, ga, gb, w_proj_a[l].astype(BF16),
               w_proj_b[l].astype(BF16), w_out[l].astype(BF16))
    return _ffn(h, mods, 6, g_ff2[l], w2_gate[l].astype(BF16), w2_up[l].astype(BF16),
                w2_down[l].astype(BF16), g_final, final_norm=True)
```

```python
import functools
import math

import jax
import jax.numpy as jnp
from jax import lax
from jax.experimental import pallas as pl
from jax.experimental.pallas import tpu as pltpu

D_MODEL = 1024
CHUNK = 64
EPS = 1e-6
DA_HEADS = 8
DA_HEAD_DIM = 64
ROPE_THETA = 10000.0
ML_HEADS = 4
ML_HEAD_DIM = 256
CONV_WIDTH = 4
D_FF = 2816
FFN_RES = 0.5
N_MOD = 9
LAMBDA_INIT = 0.8 - 0.6 * math.exp(-0.3 * 0)

V7X_LANES = 128
V7X_SUBLANES = 8
V7X_VMEM_LIMIT_BYTES = 56 * 1024 * 1024

ATTN_BLOCK = 512

BF16 = jnp.bfloat16
F32 = jnp.float32


def _resident(shape):
    nd = len(shape)
    return pl.BlockSpec(shape, lambda *_: (0,) * nd, pipeline_mode=pl.Buffered(1))


def _params(*sem):
    return pltpu.CompilerParams(dimension_semantics=sem,
                                vmem_limit_bytes=V7X_VMEM_LIMIT_BYTES)


def _mod_spec(k):
    return pl.BlockSpec((None, None, 1, D_MODEL), lambda b, i: (k, b, 0, 0))


def _rms_mod(x, g, shift, scale):
    ms = jnp.mean(x * x, axis=-1, keepdims=True)
    xn = x * lax.rsqrt(ms + EPS) * g
    return xn * (1.0 + scale) + shift


def _sigmoid(x):
    return 1.0 / (1.0 + jnp.exp(-x))


def _ada_kernel(c_ref, w_ref, b_ref, o_ref):
    c = c_ref[...]
    a = c * _sigmoid(c)
    o_ref[...] = jnp.dot(a, w_ref[...], preferred_element_type=F32) + b_ref[...]


def _ada(c, w_ada, b_ada):
    B = c.shape[0]
    return pl.pallas_call(
        _ada_kernel,
        out_shape=jax.ShapeDtypeStruct((N_MOD, B, D_MODEL), F32),
        grid=(N_MOD,),
        in_specs=[
            pl.BlockSpec((B, D_MODEL), lambda j: (0, 0)),
            pl.BlockSpec((D_MODEL, D_MODEL), lambda j: (0, j)),
            pl.BlockSpec((None, 1, D_MODEL), lambda j: (j, 0, 0)),
        ],
        out_specs=pl.BlockSpec((None, B, D_MODEL), lambda j: (j, 0, 0)),
        compiler_params=_params("arbitrary"),
        name="ada",
    )(c, w_ada, b_ada.reshape(N_MOD, 1, D_MODEL))


def _ffn_kernel(h_ref, sh_ref, sc_ref, gt_ref, g_ref, wg_ref, wu_ref, wd_ref,
                gfin_ref, o_ref, *, final_norm):
    x = h_ref[...]
    u = _rms_mod(x, g_ref[...], sh_ref[...], sc_ref[...]).astype(BF16)
    gate = jnp.dot(u, wg_ref[...], preferred_element_type=F32)
    up = jnp.dot(u, wu_ref[...], preferred_element_type=F32)
    a = (gate * _sigmoid(gate) * up).astype(BF16)
    y = jnp.dot(a, wd_ref[...], preferred_element_type=F32)
    out = x + FFN_RES * gt_ref[...] * y
    if final_norm:
        ms = jnp.mean(out * out, axis=-1, keepdims=True)
        out = out * lax.rsqrt(ms + EPS) * gfin_ref[...]
    o_ref[...] = out


def _ffn(h, mods, k0, g, wg, wu, wd, g_final, *, final_norm, tm=512):
    B, S, D = h.shape
    row = pl.BlockSpec((None, tm, D), lambda b, i: (b, i, 0))
    return pl.pallas_call(
        functools.partial(_ffn_kernel, final_norm=final_norm),
        out_shape=jax.ShapeDtypeStruct((B, S, D), F32),
        grid=(B, S // tm),
        in_specs=[row, _mod_spec(k0), _mod_spec(k0 + 1), _mod_spec(k0 + 2),
                  _resident((1, D)), _resident((D, D_FF)), _resident((D, D_FF)),
                  _resident((D_FF, D)), _resident((1, D))],
        out_specs=row,
        compiler_params=_params("parallel", "parallel"),
        name="ffn_final" if final_norm else "ffn",
    )(h, mods, mods, mods, g.reshape(1, D), wg, wu, wd, g_final.reshape(1, D))


def _rope_rows(t, cos, sin_signed):
    half = DA_HEAD_DIM // 2
    partner = jnp.concatenate([t[half:], t[:half]], axis=0)
    return t * cos + partner * sin_signed


def _rope_lanes(t, cos, sin_signed, first_half):
    outs = []
    for gidx in range(t.shape[1] // V7X_LANES):
        x = t[:, gidx * V7X_LANES:(gidx + 1) * V7X_LANES]
        partner = jnp.where(first_half,
                            pltpu.roll(x, V7X_LANES - DA_HEAD_DIM // 2, 1),
                            pltpu.roll(x, DA_HEAD_DIM // 2, 1))
        outs.append(x * cos + partner * sin_signed)
    return outs


def _proj_kernel(h_ref, sh_ref, sc_ref, g_ref, wqt_ref, wk_ref, wvt_ref, wr_ref,
                 cos_ref, sin_ref, cost_ref, sint_ref,
                 qt_ref, k1_ref, k2_ref, vt_ref, xm_ref, op_ref, ga_ref, gb_ref):
    x = h_ref[...]
    u = _rms_mod(x, g_ref[...], sh_ref[...], sc_ref[...]).astype(BF16)
    nt = (((1,), (1,)), ((), ()))

    q_scale = DA_HEAD_DIM ** -0.5 * math.log2(math.e)
    qt = lax.dot_general(wqt_ref[...], u, nt, preferred_element_type=F32)
    cost = cost_ref[...]
    sint = sint_ref[...]
    for gidx in range(D_MODEL // DA_HEAD_DIM):
        rows = slice(gidx * DA_HEAD_DIM, (gidx + 1) * DA_HEAD_DIM)
        qt_ref[rows, :] = (_rope_rows(qt[rows], cost, sint) * q_scale).astype(BF16)

    lane = lax.broadcasted_iota(jnp.int32, (x.shape[0], V7X_LANES), 1)
    first_half = (lane % DA_HEAD_DIM) < (DA_HEAD_DIM // 2)
    comp0 = lane < DA_HEAD_DIM
    k = jnp.dot(u, wk_ref[...], preferred_element_type=F32)
    for gidx, t in enumerate(_rope_lanes(k, cos_ref[...], sin_ref[...], first_half)):
        sl = slice(gidx * V7X_LANES, (gidx + 1) * V7X_LANES)
        k1_ref[:, sl] = jnp.where(comp0, t, 0.0).astype(BF16)
        k2_ref[:, sl] = jnp.where(comp0, 0.0, t).astype(BF16)

    vt_ref[...] = lax.dot_general(wvt_ref[...], u, nt,
                                  preferred_element_type=F32).astype(BF16)
    for s, ref in enumerate((xm_ref, op_ref, ga_ref, gb_ref)):
        ref[...] = jnp.dot(u, wr_ref[:, s * D_MODEL:(s + 1) * D_MODEL],
                           preferred_element_type=F32)


def _proj(h, mods, g, wqt, wk, wvt, wr, tables, *, tm):
    B, S, D = h.shape
    cos, sin, cost, sint = tables
    row = pl.BlockSpec((None, tm, D), lambda b, i: (b, i, 0))
    col = pl.BlockSpec((None, None, D, tm), lambda b, i: (b, i, 0, 0))
    tab = pl.BlockSpec((tm, V7X_LANES), lambda b, i: (i, 0))
    tabt = pl.BlockSpec((DA_HEAD_DIM, tm), lambda b, i: (0, i))
    bf = jax.ShapeDtypeStruct((B, S, D), BF16)
    bft = jax.ShapeDtypeStruct((B, S // tm, D, tm), BF16)
    f32 = jax.ShapeDtypeStruct((B, S, D), F32)
    return pl.pallas_call(
        _proj_kernel,
        out_shape=(bft, bf, bf, bft, f32, f32, f32, f32),
        grid=(B, S // tm),
        in_specs=[row, _mod_spec(3), _mod_spec(4), _resident((1, D)),
                  _resident((D, D)), _resident((D, D)), _resident((D, D)),
                  _resident((D, 4 * D)), tab, tab, tabt, tabt],
        out_specs=(col, row, row, col, row, row, row, row),
        compiler_params=_params("parallel", "parallel"),
        name="mixer_proj",
    )(h, mods, mods, g.reshape(1, D), wqt, wk, wvt, wr, cos, sin, cost, sint)


def _attn_kernel(lq1_ref, lk1_ref, lq2_ref, lk2_ref, gsub_ref,
                 qt_ref, k1_ref, k2_ref, vt_ref, o_ref, sa_ref, sb_ref, *, tq):
    qi = pl.program_id(2)
    qt = qt_ref[...]

    tk = tq // 2
    s_bufs = (sa_ref, sb_ref)

    def scores(jj, half, s_ref):
        start = pl.multiple_of(jj * tq + half * tk, tk)
        for c, k_ref in enumerate((k1_ref, k2_ref)):
            s_ref[c] = jnp.dot(k_ref[pl.ds(start, tk), :], qt,
                               preferred_element_type=F32)

    def softmax_pv(jj, half, s_ref, carry, masked):
        vt = vt_ref[jj, :, half * tk:(half + 1) * tk]
        new = []
        for c in range(2):
            m, l, acc = carry[3 * c:3 * c + 3]
            s = s_ref[c]
            if masked:
                kc = (lax.broadcasted_iota(jnp.int32, (tk, tq), 0) + half * tk) // CHUNK
                qc = lax.broadcasted_iota(jnp.int32, (tk, tq), 1) // CHUNK
                s = jnp.where(kc <= qc, s, -jnp.inf)
            m_new = jnp.maximum(m, jnp.max(s, axis=0, keepdims=True))
            alpha = jnp.exp2(m - m_new)
            p = jnp.exp2(s - m_new)
            l = alpha * l + jnp.sum(p, axis=0, keepdims=True)
            acc = alpha * acc + jnp.dot(vt, p.astype(BF16),
                                        preferred_element_type=F32)
            new += [m_new, l, acc]
        return tuple(new)

    def body(jj, carry):
        scores(jj, 1, s_bufs[1])
        carry = softmax_pv(jj, 0, s_bufs[0], carry, False)
        scores(jj + 1, 0, s_bufs[0])
        return softmax_pv(jj, 1, s_bufs[1], carry, False)

    init = []
    for _ in range(2):
        init += [jnp.full((1, tq), -jnp.inf, F32), jnp.zeros((1, tq), F32),
                 jnp.zeros((V7X_LANES, tq), F32)]
    scores(0, 0, s_bufs[0])
    carry = lax.fori_loop(0, qi, body, tuple(init))
    scores(qi, 1, s_bufs[1])
    carry = softmax_pv(qi, 0, s_bufs[0], carry, True)
    m1, l1, a1, m2, l2, a2 = softmax_pv(qi, 1, s_bufs[1], carry, True)

    lam = (jnp.exp(jnp.sum(lq1_ref[...] * lk1_ref[...], axis=-1, keepdims=True))
           - jnp.exp(jnp.sum(lq2_ref[...] * lk2_ref[...], axis=-1, keepdims=True))
           + LAMBDA_INIT)
    o = a1 * (1.0 / l1) - a2 * (lam / l2)
    ms = jnp.mean(o * o, axis=0, keepdims=True)
    o = o * lax.rsqrt(ms + EPS) * (gsub_ref[...] * (1.0 - LAMBDA_INIT))
    o_ref[...] = o.T.astype(o_ref.dtype)


def _attn(qt, k1, k2, vt, lq1, lk1, lq2, lk2, g_subln, *, tq):
    B, S, D = k1.shape
    nb = S // tq
    lam_spec = pl.BlockSpec((1, DA_HEAD_DIM), lambda b, h, i: (0, 0))
    kspec = pl.BlockSpec((None, S, V7X_LANES), lambda b, h, i: (b, 0, h))
    return pl.pallas_call(
        functools.partial(_attn_kernel, tq=tq),
        out_shape=jax.ShapeDtypeStruct((B, S, D), BF16),
        grid=(B, DA_HEADS, nb),
        in_specs=[lam_spec] * 4 + [
            pl.BlockSpec((V7X_LANES, 1), lambda b, h, i: (0, 0)),
            pl.BlockSpec((None, None, V7X_LANES, tq), lambda b, h, i: (b, i, h, 0)),
            kspec, kspec,
            pl.BlockSpec((None, nb, V7X_LANES, tq), lambda b, h, i: (b, 0, h, 0))],
        out_specs=pl.BlockSpec((None, tq, V7X_LANES), lambda b, h, i: (b, i, h)),
        scratch_shapes=[pltpu.VMEM((2, tq // 2, tq), F32)] * 2,
        compiler_params=_params("parallel", "parallel", "arbitrary"),
        name="diff_attn",
    )(lq1, lk1, lq2, lk2, g_subln.reshape(V7X_LANES, 1), qt, k1, k2, vt)


def _cumsum_rows(x):
    n = x.shape[0]
    row = lax.broadcasted_iota(jnp.int32, x.shape, 0)
    shift = 1
    while shift < n:
        x = x + jnp.where(row >= shift, pltpu.roll(x, shift, 0), 0.0)
        shift *= 2
    return x


def _lane(x, h):
    return x[:, h:h + 1]


def _mlstm_kernel(xm_ref, op_ref, cw_ref, cb_ref, wq_ref, wk_ref, wv_ref,
                  wif_ref, bif_ref, skip_ref, gn_ref, yb_ref,
                  xbuf, c_st, n_st, m_st, *, L):
    H, Dh = ML_HEADS, ML_HEAD_DIM
    halo = V7X_SUBLANES

    @pl.when(pl.program_id(1) == 0)
    def _():
        xbuf[0:halo, :] = jnp.zeros((halo, H * Dh), F32)
        c_st[...] = jnp.zeros_like(c_st)
        n_st[...] = jnp.zeros_like(n_st)
        m_st[...] = jnp.zeros_like(m_st)

    xm = xm_ref[...]
    xbuf[halo:halo + L, :] = xm
    conv = cb_ref[...] + cw_ref[CONV_WIDTH - 1:CONV_WIDTH, :] * xm
    for d in range(1, CONV_WIDTH):
        conv = conv + (cw_ref[CONV_WIDTH - 1 - d:CONV_WIDTH - d, :]
                       * xbuf[halo - d:halo - d + L, :])
    xbuf[0:halo, :] = xm[L - halo:L, :]
    xc = conv * _sigmoid(conv)

    qs, ks, vs = [], [], []
    gates = bif_ref[...]
    for h in range(H):
        sl = slice(h * Dh, (h + 1) * Dh)
        xc_h = xc[:, sl].astype(BF16)
        q = jnp.dot(xc_h, wq_ref[h], preferred_element_type=F32)
        k = jnp.dot(xc_h, wk_ref[h], preferred_element_type=F32)
        v = jnp.dot(xm[:, sl].astype(BF16), wv_ref[h], preferred_element_type=F32)
        for j, t in enumerate((q, k, v)):
            gates = gates + jnp.dot(t.astype(BF16), wif_ref[j, h],
                                    preferred_element_type=F32)
        qs.append(q)
        ks.append(k * (Dh ** -0.5))
        vs.append(v)

    ig = gates[:, :V7X_LANES]
    logf = jax.nn.log_sigmoid(gates[:, V7X_LANES:])
    b = _cumsum_rows(logf)
    a = ig - b
    a_t = a.T
    m_prev = m_st[...]
    inter = b + m_prev
    b_last = b[L - 1:L, :]
    m_new = jnp.maximum(b_last + m_prev, jnp.max(b_last + a, axis=0, keepdims=True))
    decay = jnp.exp(b_last + m_prev - m_new)
    wg = jnp.exp(b_last + a - m_new)
    m_st[...] = m_new

    tril = (lax.broadcasted_iota(jnp.int32, (L, L), 1)
            <= lax.broadcasted_iota(jnp.int32, (L, L), 0))
    nt = (((1,), (1,)), ((), ()))
    tn = (((0,), (0,)), ((), ()))
    for h in range(H):
        sl = slice(h * Dh, (h + 1) * Dh)
        q, k, v = qs[h], ks[h], vs[h]
        qb, kb, vb = q.astype(BF16), k.astype(BF16), v.astype(BF16)
        dmat = jnp.where(tril, _lane(b, h) + a_t[h:h + 1, :], -jnp.inf)
        m_t = jnp.maximum(jnp.max(dmat, axis=-1, keepdims=True), _lane(inter, h))
        w = jnp.exp(dmat - m_t)
        sc = lax.dot_general(qb, kb, nt, preferred_element_type=F32) * w
        w_inter = jnp.exp(_lane(inter, h) - m_t)
        c_prev = c_st[h]
        n_prev = n_st[h]
        num = (jnp.dot(sc.astype(BF16), vb, preferred_element_type=F32)
               + w_inter * jnp.dot(qb, c_prev.astype(BF16), preferred_element_type=F32))
        den = (jnp.sum(sc, axis=-1, keepdims=True)
               + w_inter * jnp.sum(q * n_prev, axis=-1, keepdims=True))
        hcell = num / jnp.maximum(jnp.abs(den), jnp.exp(-m_t))

        wk = _lane(wg, h) * k
        dec = _lane(decay, h)
        c_st[h] = dec * c_prev + lax.dot_general(wk.astype(BF16), vb, tn,
                                                 preferred_element_type=F32)
        n_st[h] = dec * n_prev + jnp.sum(wk, axis=0, keepdims=True)

        ms = jnp.mean(hcell * hcell, axis=-1, keepdims=True)
        hn = hcell * lax.rsqrt(ms + EPS) * gn_ref[:, sl]
        y = (hn + skip_ref[:, sl] * xc[:, sl]) * _sigmoid(op_ref[:, sl])
        yb_ref[:, sl] = y.astype(yb_ref.dtype)


def _mlstm(xm, o_pre, conv_w, conv_b, wq, wk, wv, wif, bif, ml_skip, g_mlnorm, *, L=256):
    B, S, D = xm.shape
    H, Dh = ML_HEADS, ML_HEAD_DIM
    row = pl.BlockSpec((None, L, D), lambda b, i: (b, i, 0))
    return pl.pallas_call(
        functools.partial(_mlstm_kernel, L=L),
        out_shape=jax.ShapeDtypeStruct((B, S, D), BF16),
        grid=(B, S // L),
        in_specs=[row, row, _resident((CONV_WIDTH, D)), _resident((1, D)),
                  _resident((H, Dh, Dh)), _resident((H, Dh, Dh)), _resident((H, Dh, Dh)),
                  _resident((3, H, Dh, 2 * V7X_LANES)), _resident((1, 2 * V7X_LANES)),
                  _resident((1, D)), _resident((1, D))],
        out_specs=row,
        scratch_shapes=[pltpu.VMEM((V7X_SUBLANES + L, D), F32),
                        pltpu.VMEM((H, Dh, Dh), F32),
                        pltpu.VMEM((H, 1, Dh), F32),
                        pltpu.VMEM((1, V7X_LANES), F32)],
        compiler_params=_params("parallel", "arbitrary"),
        name="mlstm",
    )(xm, o_pre, conv_w, conv_b.reshape(1, D), wq, wk, wv, wif, bif,
      ml_skip.reshape(1, D), g_mlnorm.reshape(1, D))


def _merge_kernel(h_ref, gt_ref, ya_ref, yb_ref, ga_ref, gb_ref,
                  wa_ref, wb_ref, wo_ref, o_ref):
    pa = jnp.dot(ya_ref[...], wa_ref[...], preferred_element_type=F32)
    pb = jnp.dot(yb_ref[...], wb_ref[...], preferred_element_type=F32)
    merged = _sigmoid(ga_ref[...]) * pa + _sigmoid(gb_ref[...]) * pb
    y = jnp.dot(merged.astype(BF16), wo_ref[...], preferred_element_type=F32)
    o_ref[...] = h_ref[...] + gt_ref[...] * y


def _merge(h, mods, ya, yb, ga, gb, wa, wb, wo, *, tm=512):
    B, S, D = h.shape
    row = pl.BlockSpec((None, tm, D), lambda b, i: (b, i, 0))
    return pl.pallas_call(
        _merge_kernel,
        out_shape=jax.ShapeDtypeStruct((B, S, D), F32),
        grid=(B, S // tm),
        in_specs=[row, _mod_spec(5), row, row, row, row,
                  _resident((D, D)), _resident((D, D)), _resident((D, D))],
        out_specs=row,
        compiler_params=_params("parallel", "parallel"),
        name="merge",
    )(h, mods, ya, yb, ga, gb, wa, wb, wo)


def _rope_tables(S):
    inv = ROPE_THETA ** (-jnp.arange(0, DA_HEAD_DIM, 2, dtype=F32) / DA_HEAD_DIM)
    ang = jnp.arange(S, dtype=F32)[:, None] * inv[None, :]
    cos, sin = jnp.cos(ang), jnp.sin(ang)
    cos_g = jnp.concatenate([cos, cos], axis=-1)
    sin_g = jnp.concatenate([-sin, sin], axis=-1)
    reps = V7X_LANES // DA_HEAD_DIM
    return (jnp.tile(cos_g, (1, reps)), jnp.tile(sin_g, (1, reps)), cos_g.T, sin_g.T)


def _gate_weights(w_if, b_if):
    H = ML_HEADS
    pad = V7X_LANES - H
    wi = jnp.pad(w_if[..., :H], ((0, 0),) * 3 + ((0, pad),))
    wf = jnp.pad(w_if[..., H:], ((0, 0),) * 3 + ((0, pad),))
    bi = jnp.pad(b_if[:H], (0, pad))
    bf = jnp.pad(b_if[H:], (0, pad))
    return (jnp.concatenate([wi, wf], axis=-1).astype(BF16),
            jnp.concatenate([bi, bf]).reshape(1, 2 * V7X_LANES))


def kernel(x, c, w_ada, b_ada, g_ff1, w1_gate, w1_up, w1_down, g_mix, w_in,
           lambda_q1, lambda_k1, lambda_q2, lambda_k2, g_subln, conv_w, conv_b,
           w_mq, w_mk, w_mv, w_if, b_if, ml_skip, g_mlnorm, w_proj_a, w_proj_b,
           w_out, g_ff2, w2_gate, w2_up, w2_down, g_final):
    B, S, D = x.shape
    l = 0
    mods = _ada(c, w_ada[l], b_ada[l]).reshape(N_MOD, B, 1, D)
    tables = _rope_tables(S)
    wif, bif = _gate_weights(w_if[l], b_if[l])
    w_in_bf = w_in[l].astype(BF16)
    wqt = w_in_bf[:, :D].T
    wk = w_in_bf[:, D:2 * D]
    wvt = w_in_bf[:, 2 * D:3 * D].T
    wr = w_in_bf[:, 3 * D:]

    h = _ffn(x, mods, 0, g_ff1[l], w1_gate[l].astype(BF16), w1_up[l].astype(BF16),
             w1_down[l].astype(BF16), g_final, final_norm=False)
    qt, k1, k2, vt, xm, o_pre, ga, gb = _proj(h, mods, g_mix[l], wqt, wk, wvt, wr,
                                              tables, tm=ATTN_BLOCK)
    ya = _attn(qt, k1, k2, vt, lambda_q1[l].reshape(1, -1), lambda_k1[l].reshape(1, -1),
               lambda_q2[l].reshape(1, -1), lambda_k2[l].reshape(1, -1),
               g_subln[l], tq=ATTN_BLOCK)
    yb = _mlstm(xm, o_pre, conv_w[l], conv_b[l], w_mq[l].astype(BF16),
                w_mk[l].astype(BF16), w_mv[l].astype(BF16), wif, bif,
                ml_skip[l], g_mlnorm[l])
    h = _merge(h, mods, ya, yb, ga, gb, w_proj_a[l].astype(BF16),
               w_proj_b[l].astype(BF16), w_out[l].astype(BF16))
    return _ffn(h, mods, 6, g_ff2[l], w2_gate[l].astype(BF16), w2_up[l].astype(BF16),
                w2_down[l].astype(BF16), g_final, final_norm=True)
```

```python
import functools
import math

import jax
import jax.numpy as jnp
from jax import lax
from jax.experimental import pallas as pl
from jax.experimental.pallas import tpu as pltpu

D_MODEL = 1024
CHUNK = 64
EPS = 1e-6
DA_HEADS = 8
DA_HEAD_DIM = 64
ROPE_THETA = 10000.0
ML_HEADS = 4
ML_HEAD_DIM = 256
CONV_WIDTH = 4
D_FF = 2816
FFN_RES = 0.5
N_MOD = 9
LAMBDA_INIT = 0.8 - 0.6 * math.exp(-0.3 * 0)

V7X_LANES = 128
V7X_SUBLANES = 8
V7X_MXU_COLS = 256
V7X_VMEM_LIMIT_BYTES = 56 * 1024 * 1024

ATTN_BLOCK = 512

BF16 = jnp.bfloat16
F32 = jnp.float32


def _resident(shape):
    nd = len(shape)
    return pl.BlockSpec(shape, lambda *_: (0,) * nd, pipeline_mode=pl.Buffered(1))


def _params(*sem):
    return pltpu.CompilerParams(dimension_semantics=sem,
                                vmem_limit_bytes=V7X_VMEM_LIMIT_BYTES)


def _mod_spec(k):
    return pl.BlockSpec((None, None, 1, D_MODEL), lambda b, i: (k, b, 0, 0))


def _rms_mod(x, g, shift, scale):
    ms = jnp.mean(x * x, axis=-1, keepdims=True)
    xn = x * lax.rsqrt(ms + EPS) * g
    return xn * (1.0 + scale) + shift


def _sigmoid(x):
    return 1.0 / (1.0 + jnp.exp(-x))


def _ada_kernel(c_ref, w_ref, b_ref, o_ref):
    c = c_ref[...]
    a = c * _sigmoid(c)
    o_ref[...] = jnp.dot(a, w_ref[...], preferred_element_type=F32) + b_ref[...]


def _ada(c, w_ada, b_ada):
    B = c.shape[0]
    return pl.pallas_call(
        _ada_kernel,
        out_shape=jax.ShapeDtypeStruct((N_MOD, B, D_MODEL), F32),
        grid=(N_MOD,),
        in_specs=[
            pl.BlockSpec((B, D_MODEL), lambda j: (0, 0)),
            pl.BlockSpec((D_MODEL, D_MODEL), lambda j: (0, j)),
            pl.BlockSpec((None, 1, D_MODEL), lambda j: (j, 0, 0)),
        ],
        out_specs=pl.BlockSpec((None, B, D_MODEL), lambda j: (j, 0, 0)),
        compiler_params=_params("arbitrary"),
        name="ada",
    )(c, w_ada, b_ada.reshape(N_MOD, 1, D_MODEL))


def _swiglu_residual(x, sh_ref, sc_ref, gt_ref, g_ref, wg_ref, wu_ref, wd_ref):
    u = _rms_mod(x, g_ref[...], sh_ref[...], sc_ref[...]).astype(BF16)
    gate = jnp.dot(u, wg_ref[...], preferred_element_type=F32)
    up = jnp.dot(u, wu_ref[...], preferred_element_type=F32)
    a = (gate * _sigmoid(gate) * up).astype(BF16)
    y = jnp.dot(a, wd_ref[...], preferred_element_type=F32)
    return x + FFN_RES * gt_ref[...] * y


def _ffn_kernel(h_ref, sh_ref, sc_ref, gt_ref, g_ref, wg_ref, wu_ref, wd_ref, o_ref):
    o_ref[...] = _swiglu_residual(h_ref[...], sh_ref, sc_ref, gt_ref, g_ref,
                                  wg_ref, wu_ref, wd_ref)


def _ffn_weight_specs():
    return [_resident((1, D_MODEL)), _resident((D_MODEL, D_FF)),
            _resident((D_MODEL, D_FF)), _resident((D_FF, D_MODEL))]


def _ffn(h, mods, k0, g, wg, wu, wd, *, tm=512):
    B, S, D = h.shape
    row = pl.BlockSpec((None, tm, D), lambda b, i: (b, i, 0))
    return pl.pallas_call(
        _ffn_kernel,
        out_shape=jax.ShapeDtypeStruct((B, S, D), F32),
        grid=(B, S // tm),
        in_specs=[row, _mod_spec(k0), _mod_spec(k0 + 1), _mod_spec(k0 + 2)]
        + _ffn_weight_specs(),
        out_specs=row,
        compiler_params=_params("parallel", "parallel"),
        name="ffn",
    )(h, mods, mods, mods, g.reshape(1, D), wg, wu, wd)


def _rope_rows(t, cos, sin_signed):
    half = DA_HEAD_DIM // 2
    partner = jnp.concatenate([t[half:], t[:half]], axis=0)
    return t * cos + partner * sin_signed


def _rope_lanes(t, cos, sin_signed, first_half):
    outs = []
    for gidx in range(t.shape[1] // V7X_LANES):
        x = t[:, gidx * V7X_LANES:(gidx + 1) * V7X_LANES]
        partner = jnp.where(first_half,
                            pltpu.roll(x, V7X_LANES - DA_HEAD_DIM // 2, 1),
                            pltpu.roll(x, DA_HEAD_DIM // 2, 1))
        outs.append(x * cos + partner * sin_signed)
    return outs


def _proj_kernel(h_ref, sh_ref, sc_ref, g_ref, wqt_ref, wk_ref, wvt_ref, wr_ref,
                 cos_ref, sin_ref, cost_ref, sint_ref,
                 qt_ref, k1_ref, k2_ref, vt_ref, xm_ref, op_ref, ga_ref, gb_ref):
    x = h_ref[...]
    u = _rms_mod(x, g_ref[...], sh_ref[...], sc_ref[...]).astype(BF16)
    nt = (((1,), (1,)), ((), ()))

    q_scale = DA_HEAD_DIM ** -0.5 * math.log2(math.e)
    qt = lax.dot_general(wqt_ref[...], u, nt, preferred_element_type=F32)
    cost = cost_ref[...]
    sint = sint_ref[...]
    for gidx in range(D_MODEL // DA_HEAD_DIM):
        rows = slice(gidx * DA_HEAD_DIM, (gidx + 1) * DA_HEAD_DIM)
        qt_ref[rows, :] = (_rope_rows(qt[rows], cost, sint) * q_scale).astype(BF16)

    lane = lax.broadcasted_iota(jnp.int32, (x.shape[0], V7X_LANES), 1)
    first_half = (lane % DA_HEAD_DIM) < (DA_HEAD_DIM // 2)
    comp0 = lane < DA_HEAD_DIM
    k = jnp.dot(u, wk_ref[...], preferred_element_type=F32)
    for gidx, t in enumerate(_rope_lanes(k, cos_ref[...], sin_ref[...], first_half)):
        sl = slice(gidx * V7X_LANES, (gidx + 1) * V7X_LANES)
        k1_ref[:, sl] = jnp.where(comp0, t, 0.0).astype(BF16)
        k2_ref[:, sl] = jnp.where(comp0, 0.0, t).astype(BF16)

    vt_ref[...] = lax.dot_general(wvt_ref[...], u, nt,
                                  preferred_element_type=F32).astype(BF16)
    for s, ref in enumerate((xm_ref, op_ref, ga_ref, gb_ref)):
        ref[...] = jnp.dot(u, wr_ref[:, s * D_MODEL:(s + 1) * D_MODEL],
                           preferred_element_type=F32).astype(ref.dtype)


def _proj(h, mods, g, wqt, wk, wvt, wr, tables, *, tm):
    B, S, D = h.shape
    cos, sin, cost, sint = tables
    row = pl.BlockSpec((None, tm, D), lambda b, i: (b, i, 0))
    col = pl.BlockSpec((None, None, D, tm), lambda b, i: (b, i, 0, 0))
    tab = pl.BlockSpec((tm, V7X_LANES), lambda b, i: (i, 0))
    tabt = pl.BlockSpec((DA_HEAD_DIM, tm), lambda b, i: (0, i))
    bf = jax.ShapeDtypeStruct((B, S, D), BF16)
    bft = jax.ShapeDtypeStruct((B, S // tm, D, tm), BF16)
    f32 = jax.ShapeDtypeStruct((B, S, D), F32)
    return pl.pallas_call(
        _proj_kernel,
        out_shape=(bft, bf, bf, bft, f32, bf, bf, bf),
        grid=(B, S // tm),
        in_specs=[row, _mod_spec(3), _mod_spec(4), _resident((1, D)),
                  _resident((D, D)), _resident((D, D)), _resident((D, D)),
                  _resident((D, 4 * D)), tab, tab, tabt, tabt],
        out_specs=(col, row, row, col, row, row, row, row),
        compiler_params=_params("parallel", "parallel"),
        name="mixer_proj",
    )(h, mods, mods, g.reshape(1, D), wqt, wk, wvt, wr, cos, sin, cost, sint)


def _attn_kernel(lq1_ref, lk1_ref, lq2_ref, lk2_ref, gsub_ref,
                 qt_ref, k1_ref, k2_ref, vt_ref, o_ref, s0_ref, s1_ref, *, tb):
    t = pl.program_id(2)
    qt = jnp.concatenate([qt_ref[0], qt_ref[1]], axis=1)
    wide = 2 * tb

    def scores(j, s_ref, c0=0):
        start = pl.multiple_of(j * tb, tb)
        for c, k_ref in enumerate((k1_ref, k2_ref)):
            s_ref[c, :, c0:] = jnp.dot(k_ref[pl.ds(start, tb), :], qt[:, c0:],
                                       preferred_element_type=F32)

    def softmax_pv(j, s_ref, carry, c0=0, key_off=None):
        vt = vt_ref[j]
        new = []
        for c in range(2):
            m_all, l_all, acc_all = carry[3 * c:3 * c + 3]
            m, l, acc = m_all[:, c0:], l_all[:, c0:], acc_all[:, c0:]
            s = s_ref[c, :, c0:]
            if key_off is not None:
                shape = (tb, wide - c0)
                kc = (lax.broadcasted_iota(jnp.int32, shape, 0) + key_off) // CHUNK
                qc = (lax.broadcasted_iota(jnp.int32, shape, 1) + c0) // CHUNK
                s = jnp.where(kc <= qc, s, -jnp.inf)
            m_new = jnp.maximum(m, jnp.max(s, axis=0, keepdims=True))
            alpha = jnp.exp2(m - m_new)
            p = jnp.exp2(s - m_new)
            l = alpha * l + jnp.sum(p, axis=0, keepdims=True)
            acc = alpha * acc + jnp.dot(vt, p.astype(BF16),
                                        preferred_element_type=F32)
            if c0:
                m_new = jnp.concatenate([m_all[:, :c0], m_new], axis=1)
                l = jnp.concatenate([l_all[:, :c0], l], axis=1)
                acc = jnp.concatenate([acc_all[:, :c0], acc], axis=1)
            new += [m_new, l, acc]
        return tuple(new)

    def body(u, carry):
        scores(2 * u + 1, s1_ref)
        carry = softmax_pv(2 * u, s0_ref, carry)
        scores(2 * u + 2, s0_ref)
        return softmax_pv(2 * u + 1, s1_ref, carry)

    init = []
    for _ in range(2):
        init += [jnp.full((1, wide), -jnp.inf, F32), jnp.zeros((1, wide), F32),
                 jnp.zeros((V7X_LANES, wide), F32)]
    scores(0, s0_ref)
    carry = lax.fori_loop(0, t, body, tuple(init))
    scores(2 * t + 1, s1_ref, c0=tb)
    carry = softmax_pv(2 * t, s0_ref, carry, key_off=0)
    m1, l1, a1, m2, l2, a2 = softmax_pv(2 * t + 1, s1_ref, carry, c0=tb, key_off=tb)

    lam = (jnp.exp(jnp.sum(lq1_ref[...] * lk1_ref[...], axis=-1, keepdims=True))
           - jnp.exp(jnp.sum(lq2_ref[...] * lk2_ref[...], axis=-1, keepdims=True))
           + LAMBDA_INIT)
    o = a1 * (1.0 / l1) - a2 * (lam / l2)
    ms = jnp.mean(o * o, axis=0, keepdims=True)
    o = o * lax.rsqrt(ms + EPS) * (gsub_ref[...] * (1.0 - LAMBDA_INIT))
    o_ref[...] = o.T.astype(o_ref.dtype)


def _attn(qt, k1, k2, vt, lq1, lk1, lq2, lk2, g_subln, *, tb):
    B, S, D = k1.shape
    nb = S // tb
    lam_spec = pl.BlockSpec((1, DA_HEAD_DIM), lambda b, h, i: (0, 0))
    kspec = pl.BlockSpec((None, S, V7X_LANES), lambda b, h, i: (b, 0, h))
    steps = nb // 2
    return pl.pallas_call(
        functools.partial(_attn_kernel, tb=tb),
        out_shape=jax.ShapeDtypeStruct((B, S, D), BF16),
        grid=(B, DA_HEADS, steps),
        in_specs=[lam_spec] * 4 + [
            pl.BlockSpec((V7X_LANES, 1), lambda b, h, i: (0, 0)),
            pl.BlockSpec((None, 2, V7X_LANES, tb), lambda b, h, i: (b, i, h, 0)),
            kspec, kspec,
            pl.BlockSpec((None, nb, V7X_LANES, tb), lambda b, h, i: (b, 0, h, 0))],
        out_specs=pl.BlockSpec((None, 2 * tb, V7X_LANES), lambda b, h, i: (b, i, h)),
        scratch_shapes=[pltpu.VMEM((2, tb, 2 * tb), F32)] * 2,
        compiler_params=_params("parallel", "parallel", "arbitrary"),
        name="diff_attn",
    )(lq1, lk1, lq2, lk2, g_subln.reshape(V7X_LANES, 1), qt, k1, k2, vt)


def _cumsum_rows(x):
    n = x.shape[0]
    row = lax.broadcasted_iota(jnp.int32, x.shape, 0)
    shift = 1
    while shift < n:
        x = x + jnp.where(row >= shift, pltpu.roll(x, shift, 0), 0.0)
        shift *= 2
    return x


def _lane(x, h):
    return x[:, h:h + 1]


def _mlstm_kernel(xm_ref, op_ref, cw_ref, cb_ref, wq_ref, wk_ref, wv_ref,
                  wif_ref, bif_ref, skip_ref, gn_ref, yb_ref,
                  xbuf, c_st, n_st, m_st, *, L, nseq):
    H, Dh = ML_HEADS, ML_HEAD_DIM
    halo = V7X_SUBLANES

    @pl.when(pl.program_id(1) == 0)
    def _():
        xbuf[:, 0:halo, :] = jnp.zeros((nseq, halo, H * Dh), F32)
        c_st[...] = jnp.zeros_like(c_st)
        n_st[...] = jnp.zeros_like(n_st)
        m_st[...] = jnp.zeros_like(m_st)

    def front(s):
        xm = xm_ref[s]
        xbuf[s, halo:halo + L, :] = xm
        conv = cb_ref[...] + cw_ref[CONV_WIDTH - 1:CONV_WIDTH, :] * xm
        for d in range(1, CONV_WIDTH):
            conv = conv + (cw_ref[CONV_WIDTH - 1 - d:CONV_WIDTH - d, :]
                           * xbuf[s, halo - d:halo - d + L, :])
        xbuf[s, 0:halo, :] = xm[L - halo:L, :]
        xc = conv * _sigmoid(conv)
        qs, ks, vs = [], [], []
        gates = bif_ref[...]
        for h in range(H):
            sl = slice(h * Dh, (h + 1) * Dh)
            xc_h = xc[:, sl].astype(BF16)
            q = jnp.dot(xc_h, wq_ref[h], preferred_element_type=F32)
            k = jnp.dot(xc_h, wk_ref[h], preferred_element_type=F32)
            v = jnp.dot(xm[:, sl].astype(BF16), wv_ref[h], preferred_element_type=F32)
            for j, t in enumerate((q, k, v)):
                gates = gates + jnp.dot(t.astype(BF16), wif_ref[j, h],
                                        preferred_element_type=F32)
            qs.append(q)
            ks.append(k * (Dh ** -0.5))
            vs.append(v)
        return xc, qs, ks, vs, gates

    def gate_math(s, gates):
        ig = gates[:, :V7X_LANES]
        logf = jax.nn.log_sigmoid(gates[:, V7X_LANES:])
        b = _cumsum_rows(logf)
        a = ig - b
        m_prev = m_st[s]
        b_last = b[L - 1:L, :]
        m_new = jnp.maximum(b_last + m_prev,
                            jnp.max(b_last + a, axis=0, keepdims=True))
        decay = jnp.exp(b_last + m_prev - m_new)
        wg = jnp.exp(b_last + a - m_new)
        m_st[s] = m_new
        return b, a.T, b + m_prev, decay, wg

    tril = (lax.broadcasted_iota(jnp.int32, (L, L), 1)
            <= lax.broadcasted_iota(jnp.int32, (L, L), 0))
    nt = (((1,), (1,)), ((), ()))
    tn = (((0,), (0,)), ((), ()))

    def head(s, h, xc, q, k, v, b, a_t, inter, decay, wg):
        sl = slice(h * Dh, (h + 1) * Dh)
        qb, kb, vb = q.astype(BF16), k.astype(BF16), v.astype(BF16)
        dmat = jnp.where(tril, _lane(b, h) + a_t[h:h + 1, :], -jnp.inf)
        m_t = jnp.maximum(jnp.max(dmat, axis=-1, keepdims=True), _lane(inter, h))
        w = jnp.exp(dmat - m_t)
        sc = lax.dot_general(qb, kb, nt, preferred_element_type=F32) * w
        w_inter = jnp.exp(_lane(inter, h) - m_t)
        c_prev = c_st[s, h]
        n_prev = n_st[s, h]
        num = (jnp.dot(sc.astype(BF16), vb, preferred_element_type=F32)
               + w_inter * jnp.dot(qb, c_prev.astype(BF16), preferred_element_type=F32))
        den = (jnp.sum(sc, axis=-1, keepdims=True)
               + w_inter * jnp.sum(q * n_prev, axis=-1, keepdims=True))
        hcell = num / jnp.maximum(jnp.abs(den), jnp.exp(-m_t))

        wk = _lane(wg, h) * k
        dec = _lane(decay, h)
        c_st[s, h] = dec * c_prev + lax.dot_general(wk.astype(BF16), vb, tn,
                                                    preferred_element_type=F32)
        n_st[s, h] = dec * n_prev + jnp.sum(wk, axis=0, keepdims=True)

        ms = jnp.mean(hcell * hcell, axis=-1, keepdims=True)
        hn = hcell * lax.rsqrt(ms + EPS) * gn_ref[:, sl]
        y = ((hn + skip_ref[:, sl] * xc[:, sl])
             * _sigmoid(op_ref[s, :, sl].astype(F32)))
        yb_ref[s, :, sl] = y.astype(yb_ref.dtype)

    fronts = [front(s) for s in range(nseq)]
    maths = [gate_math(s, fronts[s][4]) for s in range(nseq)]
    for h in range(H):
        for s in range(nseq):
            xc, qs, ks, vs, _ = fronts[s]
            head(s, h, xc, qs[h], ks[h], vs[h], *maths[s])


def _mlstm(xm, o_pre, conv_w, conv_b, wq, wk, wv, wif, bif, ml_skip, g_mlnorm,
           *, L=256, nseq=2):
    B, S, D = xm.shape
    H, Dh = ML_HEADS, ML_HEAD_DIM
    row = pl.BlockSpec((nseq, L, D), lambda b, i: (b, i, 0))
    return pl.pallas_call(
        functools.partial(_mlstm_kernel, L=L, nseq=nseq),
        out_shape=jax.ShapeDtypeStruct((B, S, D), BF16),
        grid=(B // nseq, S // L),
        in_specs=[row, row, _resident((CONV_WIDTH, D)), _resident((1, D)),
                  _resident((H, Dh, Dh)), _resident((H, Dh, Dh)), _resident((H, Dh, Dh)),
                  _resident((3, H, Dh, 2 * V7X_LANES)), _resident((1, 2 * V7X_LANES)),
                  _resident((1, D)), _resident((1, D))],
        out_specs=row,
        scratch_shapes=[pltpu.VMEM((nseq, V7X_SUBLANES + L, D), F32),
                        pltpu.VMEM((nseq, H, Dh, Dh), F32),
                        pltpu.VMEM((nseq, H, 1, Dh), F32),
                        pltpu.VMEM((nseq, 1, V7X_LANES), F32)],
        compiler_params=_params("parallel", "arbitrary"),
        name="mlstm",
    )(xm, o_pre, conv_w, conv_b.reshape(1, D), wq, wk, wv, wif, bif,
      ml_skip.reshape(1, D), g_mlnorm.reshape(1, D))


def _merge_ffn_kernel(h_ref, gt2_ref, ya_ref, yb_ref, ga_ref, gb_ref,
                      wa_ref, wb_ref, wo_ref, sh_ref, sc_ref, gt_ref, g_ref,
                      wg_ref, wu_ref, wd_ref, gfin_ref, o_ref):
    pa = jnp.dot(ya_ref[...], wa_ref[...], preferred_element_type=F32)
    pb = jnp.dot(yb_ref[...], wb_ref[...], preferred_element_type=F32)
    merged = (_sigmoid(ga_ref[...].astype(F32)) * pa
              + _sigmoid(gb_ref[...].astype(F32)) * pb)
    y = jnp.dot(merged.astype(BF16), wo_ref[...], preferred_element_type=F32)
    x = h_ref[...] + gt2_ref[...] * y
    out = _swiglu_residual(x, sh_ref, sc_ref, gt_ref, g_ref, wg_ref, wu_ref, wd_ref)
    ms = jnp.mean(out * out, axis=-1, keepdims=True)
    o_ref[...] = out * lax.rsqrt(ms + EPS) * gfin_ref[...]


def _merge_ffn(h, mods, ya, yb, ga, gb, wa, wb, wo, g, wg, wu, wd, g_final, *, tm=512):
    B, S, D = h.shape
    row = pl.BlockSpec((None, tm, D), lambda b, i: (b, i, 0))
    return pl.pallas_call(
        _merge_ffn_kernel,
        out_shape=jax.ShapeDtypeStruct((B, S, D), F32),
        grid=(B, S // tm),
        in_specs=[row, _mod_spec(5), row, row, row, row,
                  _resident((D, D)), _resident((D, D)), _resident((D, D)),
                  _mod_spec(6), _mod_spec(7), _mod_spec(8)]
        + _ffn_weight_specs() + [_resident((1, D))],
        out_specs=row,
        compiler_params=_params("parallel", "parallel"),
        name="merge_ffn",
    )(h, mods, ya, yb, ga, gb, wa, wb, wo, mods, mods, mods,
      g.reshape(1, D), wg, wu, wd, g_final.reshape(1, D))


def _rope_tables(S):
    inv = ROPE_THETA ** (-jnp.arange(0, DA_HEAD_DIM, 2, dtype=F32) / DA_HEAD_DIM)
    ang = jnp.arange(S, dtype=F32)[:, None] * inv[None, :]
    cos, sin = jnp.cos(ang), jnp.sin(ang)
    cos_g = jnp.concatenate([cos, cos], axis=-1)
    sin_g = jnp.concatenate([-sin, sin], axis=-1)
    reps = V7X_LANES // DA_HEAD_DIM
    return (jnp.tile(cos_g, (1, reps)), jnp.tile(sin_g, (1, reps)), cos_g.T, sin_g.T)


def _gate_weights(w_if, b_if):
    H = ML_HEADS
    pad = V7X_LANES - H
    wi = jnp.pad(w_if[..., :H], ((0, 0),) * 3 + ((0, pad),))
    wf = jnp.pad(w_if[..., H:], ((0, 0),) * 3 + ((0, pad),))
    bi = jnp.pad(b_if[:H], (0, pad))
    bf = jnp.pad(b_if[H:], (0, pad))
    return (jnp.concatenate([wi, wf], axis=-1).astype(BF16),
            jnp.concatenate([bi, bf]).reshape(1, 2 * V7X_LANES))


def kernel(x, c, w_ada, b_ada, g_ff1, w1_gate, w1_up, w1_down, g_mix, w_in,
           lambda_q1, lambda_k1, lambda_q2, lambda_k2, g_subln, conv_w, conv_b,
           w_mq, w_mk, w_mv, w_if, b_if, ml_skip, g_mlnorm, w_proj_a, w_proj_b,
           w_out, g_ff2, w2_gate, w2_up, w2_down, g_final):
    B, S, D = x.shape
    l = 0
    mods = _ada(c, w_ada[l], b_ada[l]).reshape(N_MOD, B, 1, D)
    tables = _rope_tables(S)
    wif, bif = _gate_weights(w_if[l], b_if[l])
    w_in_bf = w_in[l].astype(BF16)
    wqt = w_in_bf[:, :D].T
    wk = w_in_bf[:, D:2 * D]
    wvt = w_in_bf[:, 2 * D:3 * D].T
    wr = w_in_bf[:, 3 * D:]

    h = _ffn(x, mods, 0, g_ff1[l], w1_gate[l].astype(BF16), w1_up[l].astype(BF16),
             w1_down[l].astype(BF16))
    qt, k1, k2, vt, xm, o_pre, ga, gb = _proj(h, mods, g_mix[l], wqt, wk, wvt, wr, tables,
                                              tm=ATTN_BLOCK)
    ya = _attn(qt, k1, k2, vt, lambda_q1[l].reshape(1, -1), lambda_k1[l].reshape(1, -1),
               lambda_q2[l].reshape(1, -1), lambda_k2[l].reshape(1, -1),
               g_subln[l], tb=ATTN_BLOCK)
    yb = _mlstm(xm, o_pre, conv_w[l], conv_b[l], w_mq[l].astype(BF16),
                w_mk[l].astype(BF16), w_mv[l].astype(BF16), wif, bif,
                ml_skip[l], g_mlnorm[l])
    return _merge_ffn(h, mods, ya, yb, ga, gb, w_proj_a[l].astype(BF16),
                      w_proj_b[l].astype(BF16), w_out[l].astype(BF16), g_ff2[l],
                      w2_gate[l].astype(BF16), w2_up[l].astype(BF16),
                      w2_down[l].astype(BF16), g_final)
```

```python
import functools
import math

import jax
import jax.numpy as jnp
from jax import lax
from jax.experimental import pallas as pl
from jax.experimental.pallas import tpu as pltpu

D_MODEL = 1024
CHUNK = 64
EPS = 1e-6
DA_HEADS = 8
DA_HEAD_DIM = 64
ROPE_THETA = 10000.0
ML_HEADS = 4
ML_HEAD_DIM = 256
CONV_WIDTH = 4
D_FF = 2816
FFN_RES = 0.5
N_MOD = 9
LAMBDA_INIT = 0.8 - 0.6 * math.exp(-0.3 * 0)

V7X_LANES = 128
V7X_SUBLANES = 8
V7X_MXU_COLS = 256
V7X_VMEM_LIMIT_BYTES = 56 * 1024 * 1024

ATTN_BLOCK = 512
VT_ROWS = 128 + 16

BF16 = jnp.bfloat16
F32 = jnp.float32


def _resident(shape):
    nd = len(shape)
    return pl.BlockSpec(shape, lambda *_: (0,) * nd, pipeline_mode=pl.Buffered(1))


def _params(*sem):
    return pltpu.CompilerParams(dimension_semantics=sem,
                                vmem_limit_bytes=V7X_VMEM_LIMIT_BYTES)


def _mod_spec(k):
    return pl.BlockSpec((None, None, 1, D_MODEL), lambda b, i: (k, b, 0, 0))


def _rms_mod(x, g, shift, scale):
    ms = jnp.mean(x * x, axis=-1, keepdims=True)
    xn = x * lax.rsqrt(ms + EPS) * g
    return xn * (1.0 + scale) + shift


def _sigmoid(x):
    return 1.0 / (1.0 + jnp.exp(-x))


def _ada_kernel(c_ref, w_ref, b_ref, o_ref):
    c = c_ref[...]
    a = c * _sigmoid(c)
    o_ref[...] = jnp.dot(a, w_ref[...], preferred_element_type=F32) + b_ref[...]


def _ada(c, w_ada, b_ada):
    B = c.shape[0]
    return pl.pallas_call(
        _ada_kernel,
        out_shape=jax.ShapeDtypeStruct((N_MOD, B, D_MODEL), F32),
        grid=(N_MOD,),
        in_specs=[
            pl.BlockSpec((B, D_MODEL), lambda j: (0, 0)),
            pl.BlockSpec((D_MODEL, D_MODEL), lambda j: (0, j)),
            pl.BlockSpec((None, 1, D_MODEL), lambda j: (j, 0, 0)),
        ],
        out_specs=pl.BlockSpec((None, B, D_MODEL), lambda j: (j, 0, 0)),
        compiler_params=_params("arbitrary"),
        name="ada",
    )(c, w_ada, b_ada.reshape(N_MOD, 1, D_MODEL))


def _swiglu_residual(x, sh_ref, sc_ref, gt_ref, g_ref, wg_ref, wu_ref, wd_ref):
    u = _rms_mod(x, g_ref[...], sh_ref[...], sc_ref[...]).astype(BF16)
    gate = jnp.dot(u, wg_ref[...], preferred_element_type=F32)
    up = jnp.dot(u, wu_ref[...], preferred_element_type=F32)
    a = (gate * _sigmoid(gate) * up).astype(BF16)
    y = jnp.dot(a, wd_ref[...], preferred_element_type=F32)
    return x + FFN_RES * gt_ref[...] * y


def _ffn_kernel(h_ref, sh_ref, sc_ref, gt_ref, g_ref, wg_ref, wu_ref, wd_ref, o_ref):
    o_ref[...] = _swiglu_residual(h_ref[...], sh_ref, sc_ref, gt_ref, g_ref,
                                  wg_ref, wu_ref, wd_ref)


def _ffn_weight_specs():
    return [_resident((1, D_MODEL)), _resident((D_MODEL, D_FF)),
            _resident((D_MODEL, D_FF)), _resident((D_FF, D_MODEL))]


def _ffn(h, mods, k0, g, wg, wu, wd, *, tm=512):
    B, S, D = h.shape
    row = pl.BlockSpec((None, tm, D), lambda b, i: (b, i, 0))
    return pl.pallas_call(
        _ffn_kernel,
        out_shape=jax.ShapeDtypeStruct((B, S, D), F32),
        grid=(B, S // tm),
        in_specs=[row, _mod_spec(k0), _mod_spec(k0 + 1), _mod_spec(k0 + 2)]
        + _ffn_weight_specs(),
        out_specs=row,
        compiler_params=_params("parallel", "parallel"),
        name="ffn",
    )(h, mods, mods, mods, g.reshape(1, D), wg, wu, wd)


def _rope_rows(t, cos, sin_signed):
    half = DA_HEAD_DIM // 2
    partner = jnp.concatenate([t[half:], t[:half]], axis=0)
    return t * cos + partner * sin_signed


def _rope_lanes(t, cos, sin_signed, first_half):
    outs = []
    for gidx in range(t.shape[1] // V7X_LANES):
        x = t[:, gidx * V7X_LANES:(gidx + 1) * V7X_LANES]
        partner = jnp.where(first_half,
                            pltpu.roll(x, V7X_LANES - DA_HEAD_DIM // 2, 1),
                            pltpu.roll(x, DA_HEAD_DIM // 2, 1))
        outs.append(x * cos + partner * sin_signed)
    return outs


def _proj_kernel(h_ref, sh_ref, sc_ref, g_ref, wqt_ref, wk_ref, wvt_ref, wr_ref,
                 cos_ref, sin_ref, cost_ref, sint_ref,
                 qt_ref, k1_ref, k2_ref, vt_ref, xm_ref, op_ref, ga_ref, gb_ref):
    x = h_ref[...]
    u = _rms_mod(x, g_ref[...], sh_ref[...], sc_ref[...]).astype(BF16)
    nt = (((1,), (1,)), ((), ()))

    q_scale = DA_HEAD_DIM ** -0.5 * math.log2(math.e)
    qt = lax.dot_general(wqt_ref[...], u, nt, preferred_element_type=F32)
    cost = cost_ref[...]
    sint = sint_ref[...]
    for gidx in range(D_MODEL // DA_HEAD_DIM):
        rows = slice(gidx * DA_HEAD_DIM, (gidx + 1) * DA_HEAD_DIM)
        qt_ref[rows, :] = (_rope_rows(qt[rows], cost, sint) * q_scale).astype(BF16)

    lane = lax.broadcasted_iota(jnp.int32, (x.shape[0], V7X_LANES), 1)
    first_half = (lane % DA_HEAD_DIM) < (DA_HEAD_DIM // 2)
    comp0 = lane < DA_HEAD_DIM
    k = jnp.dot(u, wk_ref[...], preferred_element_type=F32)
    for gidx, t in enumerate(_rope_lanes(k, cos_ref[...], sin_ref[...], first_half)):
        sl = slice(gidx * V7X_LANES, (gidx + 1) * V7X_LANES)
        k1_ref[:, sl] = jnp.where(comp0, t, 0.0).astype(BF16)
        k2_ref[:, sl] = jnp.where(comp0, 0.0, t).astype(BF16)

    vt = lax.dot_general(wvt_ref[...], u, nt, preferred_element_type=F32).astype(BF16)
    extra = VT_ROWS - V7X_LANES
    ones_rows = (lax.broadcasted_iota(jnp.int32, (extra, x.shape[0]), 0) == 0).astype(BF16)
    for hd in range(DA_HEADS):
        vt_ref[hd * VT_ROWS:hd * VT_ROWS + V7X_LANES, :] = (
            vt[hd * V7X_LANES:(hd + 1) * V7X_LANES])
        vt_ref[hd * VT_ROWS + V7X_LANES:(hd + 1) * VT_ROWS, :] = ones_rows
    for s, ref in enumerate((xm_ref, op_ref, ga_ref, gb_ref)):
        ref[...] = jnp.dot(u, wr_ref[:, s * D_MODEL:(s + 1) * D_MODEL],
                           preferred_element_type=F32).astype(ref.dtype)


def _proj(h, mods, g, wqt, wk, wvt, wr, tables, *, tm):
    B, S, D = h.shape
    cos, sin, cost, sint = tables
    row = pl.BlockSpec((None, tm, D), lambda b, i: (b, i, 0))
    col = pl.BlockSpec((None, None, D, tm), lambda b, i: (b, i, 0, 0))
    tab = pl.BlockSpec((tm, V7X_LANES), lambda b, i: (i, 0))
    tabt = pl.BlockSpec((DA_HEAD_DIM, tm), lambda b, i: (0, i))
    bf = jax.ShapeDtypeStruct((B, S, D), BF16)
    bft = jax.ShapeDtypeStruct((B, S // tm, D, tm), BF16)
    vrows = DA_HEADS * VT_ROWS
    vspec = pl.BlockSpec((None, None, vrows, tm), lambda b, i: (b, i, 0, 0))
    bfv = jax.ShapeDtypeStruct((B, S // tm, vrows, tm), BF16)
    f32 = jax.ShapeDtypeStruct((B, S, D), F32)
    return pl.pallas_call(
        _proj_kernel,
        out_shape=(bft, bf, bf, bfv, f32, bf, bf, bf),
        grid=(B, S // tm),
        in_specs=[row, _mod_spec(3), _mod_spec(4), _resident((1, D)),
                  _resident((D, D)), _resident((D, D)), _resident((D, D)),
                  _resident((D, 4 * D)), tab, tab, tabt, tabt],
        out_specs=(col, row, row, vspec, row, row, row, row),
        compiler_params=_params("parallel", "parallel"),
        name="mixer_proj",
    )(h, mods, mods, g.reshape(1, D), wqt, wk, wvt, wr, cos, sin, cost, sint)


def _attn_kernel(lq1_ref, lk1_ref, lq2_ref, lk2_ref, gsub_ref,
                 qt_ref, k1_ref, k2_ref, vt_ref, o_ref, s0_ref, s1_ref,
                 m_ref, l_ref, acc_ref, *, tb):
    t = pl.program_id(2)
    qt = jnp.concatenate([qt_ref[0], qt_ref[1]], axis=1)
    wide = 2 * tb

    k_refs = (k1_ref, k2_ref)

    def scores(c, j, s_ref, c0=0):
        start = pl.multiple_of(j * tb, tb)
        s_ref[c, :, c0:] = jnp.dot(k_refs[c][pl.ds(start, tb), :], qt[:, c0:],
                                   preferred_element_type=F32)

    def softmax_pv(c, j, s_ref, c0=0, key_off=None):
        vt = vt_ref[j]
        s = s_ref[c, :, c0:]
        if key_off is not None:
            shape = (tb, wide - c0)
            kc = (lax.broadcasted_iota(jnp.int32, shape, 0) + key_off) // CHUNK
            qc = (lax.broadcasted_iota(jnp.int32, shape, 1) + c0) // CHUNK
            s = jnp.where(kc <= qc, s, -jnp.inf)
        m = m_ref[c, :, c0:]
        m_new = jnp.maximum(m, jnp.max(s, axis=0, keepdims=True))
        alpha = jnp.exp2(m - m_new)
        p = jnp.exp2(s - m_new)
        pv = jnp.dot(vt, p.astype(BF16), preferred_element_type=F32)
        m_ref[c, :, c0:] = m_new
        l_ref[c, :, c0:] = alpha * l_ref[c, :, c0:] + pv[V7X_LANES:V7X_LANES + 1]
        acc_ref[c, :, c0:] = alpha * acc_ref[c, :, c0:] + pv[:V7X_LANES]

    m_ref[...] = jnp.full(m_ref.shape, -jnp.inf, F32)
    l_ref[...] = jnp.zeros(l_ref.shape, F32)
    acc_ref[...] = jnp.zeros(acc_ref.shape, F32)
    for c in range(2):
        scores(c, 0, s0_ref)

    @pl.loop(0, t)
    def _(u):
        for cur, nxt, j in ((s0_ref, s1_ref, 2 * u), (s1_ref, s0_ref, 2 * u + 1)):
            for c in range(2):
                scores(c, j + 1, nxt)
                softmax_pv(c, j, cur)

    for c in range(2):
        scores(c, 2 * t + 1, s1_ref, c0=tb)
        softmax_pv(c, 2 * t, s0_ref, key_off=0)
    for c in range(2):
        softmax_pv(c, 2 * t + 1, s1_ref, c0=tb, key_off=tb)
    l1, a1, l2, a2 = l_ref[0], acc_ref[0], l_ref[1], acc_ref[1]

    lam = (jnp.exp(jnp.sum(lq1_ref[...] * lk1_ref[...], axis=-1, keepdims=True))
           - jnp.exp(jnp.sum(lq2_ref[...] * lk2_ref[...], axis=-1, keepdims=True))
           + LAMBDA_INIT)
    o = a1 * (1.0 / l1) - a2 * (lam / l2)
    ms = jnp.mean(o * o, axis=0, keepdims=True)
    o = o * lax.rsqrt(ms + EPS) * (gsub_ref[...] * (1.0 - LAMBDA_INIT))
    o_ref[...] = o.T.astype(o_ref.dtype)


def _attn(qt, k1, k2, vt, lq1, lk1, lq2, lk2, g_subln, *, tb):
    B, S, D = k1.shape
    nb = S // tb
    lam_spec = pl.BlockSpec((1, DA_HEAD_DIM), lambda b, h, i: (0, 0))
    kspec = pl.BlockSpec((None, S, V7X_LANES), lambda b, h, i: (b, 0, h))
    steps = nb // 2
    return pl.pallas_call(
        functools.partial(_attn_kernel, tb=tb),
        out_shape=jax.ShapeDtypeStruct((B, S, D), BF16),
        grid=(B, DA_HEADS, steps),
        in_specs=[lam_spec] * 4 + [
            pl.BlockSpec((V7X_LANES, 1), lambda b, h, i: (0, 0)),
            pl.BlockSpec((None, 2, V7X_LANES, tb), lambda b, h, i: (b, i, h, 0)),
            kspec, kspec,
            pl.BlockSpec((None, nb, VT_ROWS, tb), lambda b, h, i: (b, 0, h, 0))],
        out_specs=pl.BlockSpec((None, 2 * tb, V7X_LANES), lambda b, h, i: (b, i, h)),
        scratch_shapes=[pltpu.VMEM((2, tb, 2 * tb), F32)] * 2 + [
            pltpu.VMEM((2, 1, 2 * tb), F32), pltpu.VMEM((2, 1, 2 * tb), F32),
            pltpu.VMEM((2, V7X_LANES, 2 * tb), F32)],
        compiler_params=_params("parallel", "parallel", "arbitrary"),
        name="diff_attn",
    )(lq1, lk1, lq2, lk2, g_subln.reshape(V7X_LANES, 1), qt, k1, k2, vt)


def _cumsum_rows(x):
    n = x.shape[0]
    row = lax.broadcasted_iota(jnp.int32, x.shape, 0)
    shift = 1
    while shift < n:
        x = x + jnp.where(row >= shift, pltpu.roll(x, shift, 0), 0.0)
        shift *= 2
    return x


def _lane(x, h):
    return x[:, h:h + 1]


def _mlstm_kernel(xm_ref, op_ref, cw_ref, cb_ref, wq_ref, wk_ref, wv_ref,
                  wif_ref, bif_ref, skip_ref, gn_ref, yb_ref,
                  xbuf, c_st, n_st, m_st, *, L, nseq):
    H, Dh = ML_HEADS, ML_HEAD_DIM
    halo = V7X_SUBLANES

    @pl.when(pl.program_id(1) == 0)
    def _():
        xbuf[:, 0:halo, :] = jnp.zeros((nseq, halo, H * Dh), F32)
        c_st[...] = jnp.zeros_like(c_st)
        n_st[...] = jnp.zeros_like(n_st)
        m_st[...] = jnp.zeros_like(m_st)

    def front(s):
        xm = xm_ref[s]
        xbuf[s, halo:halo + L, :] = xm
        conv = cb_ref[...] + cw_ref[CONV_WIDTH - 1:CONV_WIDTH, :] * xm
        for d in range(1, CONV_WIDTH):
            conv = conv + (cw_ref[CONV_WIDTH - 1 - d:CONV_WIDTH - d, :]
                           * xbuf[s, halo - d:halo - d + L, :])
        xbuf[s, 0:halo, :] = xm[L - halo:L, :]
        xc = conv * _sigmoid(conv)
        qs, ks, vs = [], [], []
        gates = bif_ref[...]
        for h in range(H):
            sl = slice(h * Dh, (h + 1) * Dh)
            xc_h = xc[:, sl].astype(BF16)
            q = jnp.dot(xc_h, wq_ref[h], preferred_element_type=F32)
            k = jnp.dot(xc_h, wk_ref[h], preferred_element_type=F32)
            v = jnp.dot(xm[:, sl].astype(BF16), wv_ref[h], preferred_element_type=F32)
            for j, t in enumerate((q, k, v)):
                gates = gates + jnp.dot(t.astype(BF16), wif_ref[j, h],
                                        preferred_element_type=F32)
            qs.append(q)
            ks.append(k * (Dh ** -0.5))
            vs.append(v)
        return xc, qs, ks, vs, gates

    def gate_math(s, gates):
        ig = gates[:, :V7X_LANES]
        logf = jax.nn.log_sigmoid(gates[:, V7X_LANES:])
        b = _cumsum_rows(logf)
        a = ig - b
        m_prev = m_st[s]
        b_last = b[L - 1:L, :]
        m_new = jnp.maximum(b_last + m_prev,
                            jnp.max(b_last + a, axis=0, keepdims=True))
        decay = jnp.exp(b_last + m_prev - m_new)
        wg = jnp.exp(b_last + a - m_new)
        m_st[s] = m_new
        return b, a.T, b + m_prev, decay, wg

    tril = (lax.broadcasted_iota(jnp.int32, (L, L), 1)
            <= lax.broadcasted_iota(jnp.int32, (L, L), 0))
    nt = (((1,), (1,)), ((), ()))
    tn = (((0,), (0,)), ((), ()))

    def head(s, h, xc, q, k, v, b, a_t, inter, decay, wg):
        sl = slice(h * Dh, (h + 1) * Dh)
        qb, kb, vb = q.astype(BF16), k.astype(BF16), v.astype(BF16)
        dmat = jnp.where(tril, _lane(b, h) + a_t[h:h + 1, :], -jnp.inf)
        m_t = jnp.maximum(jnp.max(dmat, axis=-1, keepdims=True), _lane(inter, h))
        w = jnp.exp(dmat - m_t)
        sc = lax.dot_general(qb, kb, nt, preferred_element_type=F32) * w
        w_inter = jnp.exp(_lane(inter, h) - m_t)
        c_prev = c_st[s, h]
        n_prev = n_st[s, h]
        num = (jnp.dot(sc.astype(BF16), vb, preferred_element_type=F32)
               + w_inter * jnp.dot(qb, c_prev.astype(BF16), preferred_element_type=F32))
        den = (jnp.sum(sc, axis=-1, keepdims=True)
               + w_inter * jnp.sum(q * n_prev, axis=-1, keepdims=True))
        hcell = num / jnp.maximum(jnp.abs(den), jnp.exp(-m_t))

        wk = _lane(wg, h) * k
        dec = _lane(decay, h)
        c_st[s, h] = dec * c_prev + lax.dot_general(wk.astype(BF16), vb, tn,
                                                    preferred_element_type=F32)
        n_st[s, h] = dec * n_prev + jnp.sum(wk, axis=0, keepdims=True)

        ms = jnp.mean(hcell * hcell, axis=-1, keepdims=True)
        hn = hcell * lax.rsqrt(ms + EPS) * gn_ref[:, sl]
        y = ((hn + skip_ref[:, sl] * xc[:, sl])
             * _sigmoid(op_ref[s, :, sl].astype(F32)))
        yb_ref[s, :, sl] = y.astype(yb_ref.dtype)

    fronts = [front(s) for s in range(nseq)]
    maths = [gate_math(s, fronts[s][4]) for s in range(nseq)]
    for h in range(H):
        for s in range(nseq):
            xc, qs, ks, vs, _ = fronts[s]
            head(s, h, xc, qs[h], ks[h], vs[h], *maths[s])


def _mlstm(xm, o_pre, conv_w, conv_b, wq, wk, wv, wif, bif, ml_skip, g_mlnorm,
           *, L=256, nseq=2):
    B, S, D = xm.shape
    H, Dh = ML_HEADS, ML_HEAD_DIM
    row = pl.BlockSpec((nseq, L, D), lambda b, i: (b, i, 0))
    return pl.pallas_call(
        functools.partial(_mlstm_kernel, L=L, nseq=nseq),
        out_shape=jax.ShapeDtypeStruct((B, S, D), BF16),
        grid=(B // nseq, S // L),
        in_specs=[row, row, _resident((CONV_WIDTH, D)), _resident((1, D)),
                  _resident((H, Dh, Dh)), _resident((H, Dh, Dh)), _resident((H, Dh, Dh)),
                  _resident((3, H, Dh, 2 * V7X_LANES)), _resident((1, 2 * V7X_LANES)),
                  _resident((1, D)), _resident((1, D))],
        out_specs=row,
        scratch_shapes=[pltpu.VMEM((nseq, V7X_SUBLANES + L, D), F32),
                        pltpu.VMEM((nseq, H, Dh, Dh), F32),
                        pltpu.VMEM((nseq, H, 1, Dh), F32),
                        pltpu.VMEM((nseq, 1, V7X_LANES), F32)],
        compiler_params=_params("parallel", "arbitrary"),
        name="mlstm",
    )(xm, o_pre, conv_w, conv_b.reshape(1, D), wq, wk, wv, wif, bif,
      ml_skip.reshape(1, D), g_mlnorm.reshape(1, D))


def _merge_ffn_kernel(h_ref, gt2_ref, ya_ref, yb_ref, ga_ref, gb_ref,
                      wa_ref, wb_ref, wo_ref, sh_ref, sc_ref, gt_ref, g_ref,
                      wg_ref, wu_ref, wd_ref, gfin_ref, o_ref):
    pa = jnp.dot(ya_ref[...], wa_ref[...], preferred_element_type=F32)
    pb = jnp.dot(yb_ref[...], wb_ref[...], preferred_element_type=F32)
    merged = (_sigmoid(ga_ref[...].astype(F32)) * pa
              + _sigmoid(gb_ref[...].astype(F32)) * pb)
    y = jnp.dot(merged.astype(BF16), wo_ref[...], preferred_element_type=F32)
    x = h_ref[...] + gt2_ref[...] * y
    out = _swiglu_residual(x, sh_ref, sc_ref, gt_ref, g_ref, wg_ref, wu_ref, wd_ref)
    ms = jnp.mean(out * out, axis=-1, keepdims=True)
    o_ref[...] = out * lax.rsqrt(ms + EPS) * gfin_ref[...]


def _merge_ffn(h, mods, ya, yb, ga, gb, wa, wb, wo, g, wg, wu, wd, g_final, *, tm=512):
    B, S, D = h.shape
    row = pl.BlockSpec((None, tm, D), lambda b, i: (b, i, 0))
    return pl.pallas_call(
        _merge_ffn_kernel,
        out_shape=jax.ShapeDtypeStruct((B, S, D), F32),
        grid=(B, S // tm),
        in_specs=[row, _mod_spec(5), row, row, row, row,
                  _resident((D, D)), _resident((D, D)), _resident((D, D)),
                  _mod_spec(6), _mod_spec(7), _mod_spec(8)]
        + _ffn_weight_specs() + [_resident((1, D))],
        out_specs=row,
        compiler_params=_params("parallel", "parallel"),
        name="merge_ffn",
    )(h, mods, ya, yb, ga, gb, wa, wb, wo, mods, mods, mods,
      g.reshape(1, D), wg, wu, wd, g_final.reshape(1, D))


def _rope_tables(S):
    inv = ROPE_THETA ** (-jnp.arange(0, DA_HEAD_DIM, 2, dtype=F32) / DA_HEAD_DIM)
    ang = jnp.arange(S, dtype=F32)[:, None] * inv[None, :]
    cos, sin = jnp.cos(ang), jnp.sin(ang)
    cos_g = jnp.concatenate([cos, cos], axis=-1)
    sin_g = jnp.concatenate([-sin, sin], axis=-1)
    reps = V7X_LANES // DA_HEAD_DIM
    return (jnp.tile(cos_g, (1, reps)), jnp.tile(sin_g, (1, reps)), cos_g.T, sin_g.T)


def _gate_weights(w_if, b_if):
    H = ML_HEADS
    pad = V7X_LANES - H
    wi = jnp.pad(w_if[..., :H], ((0, 0),) * 3 + ((0, pad),))
    wf = jnp.pad(w_if[..., H:], ((0, 0),) * 3 + ((0, pad),))
    bi = jnp.pad(b_if[:H], (0, pad))
    bf = jnp.pad(b_if[H:], (0, pad))
    return (jnp.concatenate([wi, wf], axis=-1).astype(BF16),
            jnp.concatenate([bi, bf]).reshape(1, 2 * V7X_LANES))


def kernel(x, c, w_ada, b_ada, g_ff1, w1_gate, w1_up, w1_down, g_mix, w_in,
           lambda_q1, lambda_k1, lambda_q2, lambda_k2, g_subln, conv_w, conv_b,
           w_mq, w_mk, w_mv, w_if, b_if, ml_skip, g_mlnorm, w_proj_a, w_proj_b,
           w_out, g_ff2, w2_gate, w2_up, w2_down, g_final):
    B, S, D = x.shape
    l = 0
    mods = _ada(c, w_ada[l], b_ada[l]).reshape(N_MOD, B, 1, D)
    tables = _rope_tables(S)
    wif, bif = _gate_weights(w_if[l], b_if[l])
    w_in_bf = w_in[l].astype(BF16)
    wqt = w_in_bf[:, :D].T
    wk = w_in_bf[:, D:2 * D]
    wvt = w_in_bf[:, 2 * D:3 * D].T
    wr = w_in_bf[:, 3 * D:]

    h = _ffn(x, mods, 0, g_ff1[l], w1_gate[l].astype(BF16), w1_up[l].astype(BF16),
             w1_down[l].astype(BF16))
    qt, k1, k2, vt, xm, o_pre, ga, gb = _proj(h, mods, g_mix[l], wqt, wk, wvt, wr, tables,
                                              tm=ATTN_BLOCK)
    ya = _attn(qt, k1, k2, vt, lambda_q1[l].reshape(1, -1), lambda_k1[l].reshape(1, -1),
               lambda_q2[l].reshape(1, -1), lambda_k2[l].reshape(1, -1),
               g_subln[l], tb=ATTN_BLOCK)
    yb = _mlstm(xm, o_pre, conv_w[l], conv_b[l], w_mq[l].astype(BF16),
                w_mk[l].astype(BF16), w_mv[l].astype(BF16), wif, bif,
                ml_skip[l], g_mlnorm[l])
    return _merge_ffn(h, mods, ya, yb, ga, gb, w_proj_a[l].astype(BF16),
                      w_proj_b[l].astype(BF16), w_out[l].astype(BF16), g_ff2[l],
                      w2_gate[l].astype(BF16), w2_up[l].astype(BF16),
                      w2_down[l].astype(BF16), g_final)
```

```python
import functools
import math

import jax
import jax.numpy as jnp
from jax import lax
from jax.experimental import pallas as pl
from jax.experimental.pallas import tpu as pltpu

D_MODEL = 1024
CHUNK = 64
EPS = 1e-6
DA_HEADS = 8
DA_HEAD_DIM = 64
ROPE_THETA = 10000.0
ML_HEADS = 4
ML_HEAD_DIM = 256
CONV_WIDTH = 4
D_FF = 2816
FFN_RES = 0.5
N_MOD = 9
LAMBDA_INIT = 0.8 - 0.6 * math.exp(-0.3 * 0)

V7X_LANES = 128
V7X_SUBLANES = 8
V7X_MXU_COLS = 256
V7X_VMEM_LIMIT_BYTES = 56 * 1024 * 1024

ATTN_BLOCK = 512
VT_ROWS = 128 + 16

BF16 = jnp.bfloat16
F32 = jnp.float32


def _resident(shape):
    nd = len(shape)
    return pl.BlockSpec(shape, lambda *_: (0,) * nd, pipeline_mode=pl.Buffered(1))


def _params(*sem):
    return pltpu.CompilerParams(dimension_semantics=sem,
                                vmem_limit_bytes=V7X_VMEM_LIMIT_BYTES)


def _mod_spec(k):
    return pl.BlockSpec((None, None, 1, D_MODEL), lambda b, i: (k, b, 0, 0))


def _rms_mod(x, g, shift, scale):
    ms = jnp.mean(x * x, axis=-1, keepdims=True)
    xn = x * lax.rsqrt(ms + EPS) * g
    return xn * (1.0 + scale) + shift


def _sigmoid(x):
    return 1.0 / (1.0 + jnp.exp(-x))


def _ada_kernel(c_ref, w_ref, b_ref, o_ref):
    c = c_ref[...]
    a = c * _sigmoid(c)
    o_ref[...] = jnp.dot(a, w_ref[...], preferred_element_type=F32) + b_ref[...]


def _ada(c, w_ada, b_ada):
    B = c.shape[0]
    return pl.pallas_call(
        _ada_kernel,
        out_shape=jax.ShapeDtypeStruct((N_MOD, B, D_MODEL), F32),
        grid=(N_MOD,),
        in_specs=[
            pl.BlockSpec((B, D_MODEL), lambda j: (0, 0)),
            pl.BlockSpec((D_MODEL, D_MODEL), lambda j: (0, j)),
            pl.BlockSpec((None, 1, D_MODEL), lambda j: (j, 0, 0)),
        ],
        out_specs=pl.BlockSpec((None, B, D_MODEL), lambda j: (j, 0, 0)),
        compiler_params=_params("arbitrary"),
        name="ada",
    )(c, w_ada, b_ada.reshape(N_MOD, 1, D_MODEL))


def _swiglu_residual(x, sh_ref, sc_ref, gt_ref, g_ref, wg_ref, wu_ref, wd_ref):
    u = _rms_mod(x, g_ref[...], sh_ref[...], sc_ref[...]).astype(BF16)
    gate = jnp.dot(u, wg_ref[...], preferred_element_type=F32)
    up = jnp.dot(u, wu_ref[...], preferred_element_type=F32)
    a = (gate * _sigmoid(gate) * up).astype(BF16)
    y = jnp.dot(a, wd_ref[...], preferred_element_type=F32)
    return x + FFN_RES * gt_ref[...] * y


def _ffn_kernel(h_ref, sh_ref, sc_ref, gt_ref, g_ref, wg_ref, wu_ref, wd_ref, o_ref):
    o_ref[...] = _swiglu_residual(h_ref[...], sh_ref, sc_ref, gt_ref, g_ref,
                                  wg_ref, wu_ref, wd_ref)


def _ffn_weight_specs():
    return [_resident((1, D_MODEL)), _resident((D_MODEL, D_FF)),
            _resident((D_MODEL, D_FF)), _resident((D_FF, D_MODEL))]


def _ffn(h, mods, k0, g, wg, wu, wd, *, tm=512):
    B, S, D = h.shape
    row = pl.BlockSpec((None, tm, D), lambda b, i: (b, i, 0))
    return pl.pallas_call(
        _ffn_kernel,
        out_shape=jax.ShapeDtypeStruct((B, S, D), F32),
        grid=(B, S // tm),
        in_specs=[row, _mod_spec(k0), _mod_spec(k0 + 1), _mod_spec(k0 + 2)]
        + _ffn_weight_specs(),
        out_specs=row,
        compiler_params=_params("parallel", "parallel"),
        name="ffn",
    )(h, mods, mods, mods, g.reshape(1, D), wg, wu, wd)


def _rope_rows(t, cos, sin_signed):
    half = DA_HEAD_DIM // 2
    partner = jnp.concatenate([t[half:], t[:half]], axis=0)
    return t * cos + partner * sin_signed


def _rope_lanes(t, cos, sin_signed, first_half):
    outs = []
    for gidx in range(t.shape[1] // V7X_LANES):
        x = t[:, gidx * V7X_LANES:(gidx + 1) * V7X_LANES]
        partner = jnp.where(first_half,
                            pltpu.roll(x, V7X_LANES - DA_HEAD_DIM // 2, 1),
                            pltpu.roll(x, DA_HEAD_DIM // 2, 1))
        outs.append(x * cos + partner * sin_signed)
    return outs


def _proj_kernel(h_ref, sh_ref, sc_ref, g_ref, wqt_ref, wk_ref, wvt_ref, wr_ref,
                 cos_ref, sin_ref, cost_ref, sint_ref,
                 qt_ref, k1_ref, k2_ref, vt_ref, xm_ref, op_ref, ga_ref, gb_ref):
    x = h_ref[...]
    u = _rms_mod(x, g_ref[...], sh_ref[...], sc_ref[...]).astype(BF16)
    nt = (((1,), (1,)), ((), ()))

    q_scale = DA_HEAD_DIM ** -0.5 * math.log2(math.e)
    qt = lax.dot_general(wqt_ref[...], u, nt, preferred_element_type=F32)
    cost = cost_ref[...]
    sint = sint_ref[...]
    for gidx in range(D_MODEL // DA_HEAD_DIM):
        rows = slice(gidx * DA_HEAD_DIM, (gidx + 1) * DA_HEAD_DIM)
        qt_ref[rows, :] = (_rope_rows(qt[rows], cost, sint) * q_scale).astype(BF16)

    lane = lax.broadcasted_iota(jnp.int32, (x.shape[0], V7X_LANES), 1)
    first_half = (lane % DA_HEAD_DIM) < (DA_HEAD_DIM // 2)
    comp0 = lane < DA_HEAD_DIM
    k = jnp.dot(u, wk_ref[...], preferred_element_type=F32)
    for gidx, t in enumerate(_rope_lanes(k, cos_ref[...], sin_ref[...], first_half)):
        sl = slice(gidx * V7X_LANES, (gidx + 1) * V7X_LANES)
        k1_ref[:, sl] = jnp.where(comp0, t, 0.0).astype(BF16)
        k2_ref[:, sl] = jnp.where(comp0, 0.0, t).astype(BF16)

    vt = lax.dot_general(wvt_ref[...], u, nt, preferred_element_type=F32).astype(BF16)
    extra = VT_ROWS - V7X_LANES
    ones_rows = (lax.broadcasted_iota(jnp.int32, (extra, x.shape[0]), 0) == 0).astype(BF16)
    for hd in range(DA_HEADS):
        vt_ref[hd * VT_ROWS:hd * VT_ROWS + V7X_LANES, :] = (
            vt[hd * V7X_LANES:(hd + 1) * V7X_LANES])
        vt_ref[hd * VT_ROWS + V7X_LANES:(hd + 1) * VT_ROWS, :] = ones_rows
    for s, ref in enumerate((xm_ref, op_ref, ga_ref, gb_ref)):
        ref[...] = jnp.dot(u, wr_ref[:, s * D_MODEL:(s + 1) * D_MODEL],
                           preferred_element_type=F32).astype(ref.dtype)


def _proj(h, mods, g, wqt, wk, wvt, wr, tables, *, tm):
    B, S, D = h.shape
    cos, sin, cost, sint = tables
    row = pl.BlockSpec((None, tm, D), lambda b, i: (b, i, 0))
    col = pl.BlockSpec((None, None, D, tm), lambda b, i: (b, i, 0, 0))
    tab = pl.BlockSpec((tm, V7X_LANES), lambda b, i: (i, 0))
    tabt = pl.BlockSpec((DA_HEAD_DIM, tm), lambda b, i: (0, i))
    bf = jax.ShapeDtypeStruct((B, S, D), BF16)
    bft = jax.ShapeDtypeStruct((B, S // tm, D, tm), BF16)
    vrows = DA_HEADS * VT_ROWS
    vspec = pl.BlockSpec((None, None, vrows, tm), lambda b, i: (b, i, 0, 0))
    bfv = jax.ShapeDtypeStruct((B, S // tm, vrows, tm), BF16)
    f32 = jax.ShapeDtypeStruct((B, S, D), F32)
    return pl.pallas_call(
        _proj_kernel,
        out_shape=(bft, bf, bf, bfv, f32, bf, bf, bf),
        grid=(B, S // tm),
        in_specs=[row, _mod_spec(3), _mod_spec(4), _resident((1, D)),
                  _resident((D, D)), _resident((D, D)), _resident((D, D)),
                  _resident((D, 4 * D)), tab, tab, tabt, tabt],
        out_specs=(col, row, row, vspec, row, row, row, row),
        compiler_params=_params("parallel", "parallel"),
        name="mixer_proj",
    )(h, mods, mods, g.reshape(1, D), wqt, wk, wvt, wr, cos, sin, cost, sint)


def _attn_kernel(lq1_ref, lk1_ref, lq2_ref, lk2_ref, gsub_ref,
                 qt_ref, k1_ref, k2_ref, vt_ref, o_ref, s0_ref, s1_ref, s2_ref,
                 m_ref, l_ref, acc_ref, *, tb):
    t = pl.program_id(2)
    wide = 2 * tb
    lam = (jnp.exp(jnp.sum(lq1_ref[...] * lk1_ref[...], axis=-1, keepdims=True))
           - jnp.exp(jnp.sum(lq2_ref[...] * lk2_ref[...], axis=-1, keepdims=True))
           + LAMBDA_INIT)

    def head_ops(hd):
        lanes = slice(hd * V7X_LANES, (hd + 1) * V7X_LANES)
        qt = jnp.concatenate([qt_ref[0, lanes, :], qt_ref[1, lanes, :]],
                             axis=1)
        k_refs = (k1_ref, k2_ref)

        def scores(c, j, s_ref, c0=0):
            start = pl.multiple_of(j * tb, tb)
            s_ref[c, :, c0:] = jnp.dot(k_refs[c][pl.ds(start, tb), lanes], qt[:, c0:],
                                       preferred_element_type=F32)

        def softmax_pv(c, j, s_ref, c0=0, key_off=None):
            vt = vt_ref[j, hd * VT_ROWS:(hd + 1) * VT_ROWS, :]
            s = s_ref[c, :, c0:]
            if key_off is not None:
                shape = (tb, wide - c0)
                kc = (lax.broadcasted_iota(jnp.int32, shape, 0) + key_off) // CHUNK
                qc = (lax.broadcasted_iota(jnp.int32, shape, 1) + c0) // CHUNK
                s = jnp.where(kc <= qc, s, -jnp.inf)
            m = m_ref[hd, c, :, c0:]
            m_new = jnp.maximum(m, jnp.max(s, axis=0, keepdims=True))
            alpha = jnp.exp2(m - m_new)
            p = jnp.exp2(s - m_new)
            pv = jnp.dot(vt, p.astype(BF16), preferred_element_type=F32)
            m_ref[hd, c, :, c0:] = m_new
            l_ref[hd, c, :, c0:] = (alpha * l_ref[hd, c, :, c0:]
                                    + pv[V7X_LANES:V7X_LANES + 1])
            acc_ref[hd, c, :, c0:] = alpha * acc_ref[hd, c, :, c0:] + pv[:V7X_LANES]

        def first_scores(slot):
            m_ref[hd] = jnp.full(m_ref.shape[1:], -jnp.inf, F32)
            l_ref[hd] = jnp.zeros(l_ref.shape[1:], F32)
            acc_ref[hd] = jnp.zeros(acc_ref.shape[1:], F32)
            for c in range(2):
                scores(c, 0, slot)

        def full_blocks(slot_a, slot_b):
            @pl.loop(0, t)
            def _(u):
                for cur, nxt, j in ((slot_a, slot_b, 2 * u), (slot_b, slot_a, 2 * u + 1)):
                    for c in range(2):
                        scores(c, j + 1, nxt)
                        softmax_pv(c, j, cur)

        def diagonal(slot_a, slot_b, between=None):
            for c in range(2):
                scores(c, 2 * t + 1, slot_b, c0=tb)
                softmax_pv(c, 2 * t, slot_a, key_off=0)
            if between is not None:
                between()
            for c in range(2):
                softmax_pv(c, 2 * t + 1, slot_b, c0=tb, key_off=tb)
            o = (acc_ref[hd, 0] * (1.0 / l_ref[hd, 0])
                 - acc_ref[hd, 1] * (lam / l_ref[hd, 1]))
            ms = jnp.mean(o * o, axis=0, keepdims=True)
            o = o * lax.rsqrt(ms + EPS) * (gsub_ref[...] * (1.0 - LAMBDA_INIT))
            o_ref[:, lanes] = o.T.astype(o_ref.dtype)

        return first_scores, full_blocks, diagonal

    first0, full0, diag0 = head_ops(0)
    first1, full1, diag1 = head_ops(1)
    first0(s0_ref)
    full0(s0_ref, s1_ref)
    diag0(s0_ref, s1_ref, between=lambda: first1(s2_ref))
    full1(s2_ref, s1_ref)
    diag1(s2_ref, s1_ref)


def _attn(qt, k1, k2, vt, lq1, lk1, lq2, lk2, g_subln, *, tb):
    B, S, D = k1.shape
    nb = S // tb
    hp = 2
    lam_spec = pl.BlockSpec((1, DA_HEAD_DIM), lambda b, h, i: (0, 0))
    kspec = pl.BlockSpec((None, S, hp * V7X_LANES), lambda b, h, i: (b, 0, h))
    steps = nb // 2
    stat = pltpu.VMEM((hp, 2, 1, 2 * tb), F32)
    return pl.pallas_call(
        functools.partial(_attn_kernel, tb=tb),
        out_shape=jax.ShapeDtypeStruct((B, S, D), BF16),
        grid=(B, DA_HEADS // hp, steps),
        in_specs=[lam_spec] * 4 + [
            pl.BlockSpec((V7X_LANES, 1), lambda b, h, i: (0, 0)),
            pl.BlockSpec((None, 2, hp * V7X_LANES, tb), lambda b, h, i: (b, i, h, 0)),
            kspec, kspec,
            pl.BlockSpec((None, nb, hp * VT_ROWS, tb), lambda b, h, i: (b, 0, h, 0))],
        out_specs=pl.BlockSpec((None, 2 * tb, hp * V7X_LANES), lambda b, h, i: (b, i, h)),
        scratch_shapes=[pltpu.VMEM((2, tb, 2 * tb), F32)] * 3 + [
            stat, stat, pltpu.VMEM((hp, 2, V7X_LANES, 2 * tb), F32)],
        compiler_params=_params("parallel", "parallel", "arbitrary"),
        name="diff_attn",
    )(lq1, lk1, lq2, lk2, g_subln.reshape(V7X_LANES, 1), qt, k1, k2, vt)


def _cumsum_rows(x):
    n = x.shape[0]
    row = lax.broadcasted_iota(jnp.int32, x.shape, 0)
    shift = 1
    while shift < n:
        x = x + jnp.where(row >= shift, pltpu.roll(x, shift, 0), 0.0)
        shift *= 2
    return x


def _lane(x, h):
    return x[:, h:h + 1]


def _mlstm_kernel(xm_ref, op_ref, cw_ref, cb_ref, wq_ref, wk_ref, wv_ref,
                  wif_ref, bif_ref, skip_ref, gn_ref, yb_ref,
                  xbuf, c_st, n_st, m_st, *, L, nseq):
    H, Dh = ML_HEADS, ML_HEAD_DIM
    halo = V7X_SUBLANES

    @pl.when(pl.program_id(1) == 0)
    def _():
        xbuf[:, 0:halo, :] = jnp.zeros((nseq, halo, H * Dh), F32)
        c_st[...] = jnp.zeros_like(c_st)
        n_st[...] = jnp.zeros_like(n_st)
        m_st[...] = jnp.zeros_like(m_st)

    def front(s):
        xm = xm_ref[s]
        xbuf[s, halo:halo + L, :] = xm
        conv = cb_ref[...] + cw_ref[CONV_WIDTH - 1:CONV_WIDTH, :] * xm
        for d in range(1, CONV_WIDTH):
            conv = conv + (cw_ref[CONV_WIDTH - 1 - d:CONV_WIDTH - d, :]
                           * xbuf[s, halo - d:halo - d + L, :])
        xbuf[s, 0:halo, :] = xm[L - halo:L, :]
        xc = conv * _sigmoid(conv)
        qs, ks, vs = [], [], []
        gates = bif_ref[...]
        for h in range(H):
            sl = slice(h * Dh, (h + 1) * Dh)
            xc_h = xc[:, sl].astype(BF16)
            q = jnp.dot(xc_h, wq_ref[h], preferred_element_type=F32)
            k = jnp.dot(xc_h, wk_ref[h], preferred_element_type=F32)
            v = jnp.dot(xm[:, sl].astype(BF16), wv_ref[h], preferred_element_type=F32)
            for j, t in enumerate((q, k, v)):
                gates = gates + jnp.dot(t.astype(BF16), wif_ref[j, h],
                                        preferred_element_type=F32)
            qs.append(q)
            ks.append(k * (Dh ** -0.5))
            vs.append(v)
        return xc, qs, ks, vs, gates

    def gate_math(s, gates):
        ig = gates[:, :V7X_LANES]
        logf = jax.nn.log_sigmoid(gates[:, V7X_LANES:])
        b = _cumsum_rows(logf)
        a = ig - b
        m_prev = m_st[s]
        b_last = b[L - 1:L, :]
        m_new = jnp.maximum(b_last + m_prev,
                            jnp.max(b_last + a, axis=0, keepdims=True))
        decay = jnp.exp(b_last + m_prev - m_new)
        wg = jnp.exp(b_last + a - m_new)
        m_st[s] = m_new
        return b, a.T, b + m_prev, decay, wg

    tril = (lax.broadcasted_iota(jnp.int32, (L, L), 1)
            <= lax.broadcasted_iota(jnp.int32, (L, L), 0))
    nt = (((1,), (1,)), ((), ()))
    tn = (((0,), (0,)), ((), ()))

    def head(s, h, xc, q, k, v, b, a_t, inter, decay, wg):
        sl = slice(h * Dh, (h + 1) * Dh)
        qb, kb, vb = q.astype(BF16), k.astype(BF16), v.astype(BF16)
        dmat = jnp.where(tril, _lane(b, h) + a_t[h:h + 1, :], -jnp.inf)
        m_t = jnp.maximum(jnp.max(dmat, axis=-1, keepdims=True), _lane(inter, h))
        w = jnp.exp(dmat - m_t)
        sc = lax.dot_general(qb, kb, nt, preferred_element_type=F32) * w
        w_inter = jnp.exp(_lane(inter, h) - m_t)
        c_prev = c_st[s, h]
        n_prev = n_st[s, h]
        num = (jnp.dot(sc.astype(BF16), vb, preferred_element_type=F32)
               + w_inter * jnp.dot(qb, c_prev.astype(BF16), preferred_element_type=F32))
        den = (jnp.sum(sc, axis=-1, keepdims=True)
               + w_inter * jnp.sum(q * n_prev, axis=-1, keepdims=True))
        hcell = num / jnp.maximum(jnp.abs(den), jnp.exp(-m_t))

        wk = _lane(wg, h) * k
        dec = _lane(decay, h)
        c_st[s, h] = dec * c_prev + lax.dot_general(wk.astype(BF16), vb, tn,
                                                    preferred_element_type=F32)
        n_st[s, h] = dec * n_prev + jnp.sum(wk, axis=0, keepdims=True)

        ms = jnp.mean(hcell * hcell, axis=-1, keepdims=True)
        hn = hcell * lax.rsqrt(ms + EPS) * gn_ref[:, sl]
        y = ((hn + skip_ref[:, sl] * xc[:, sl])
             * _sigmoid(op_ref[s, :, sl].astype(F32)))
        yb_ref[s, :, sl] = y.astype(yb_ref.dtype)

    fronts = [front(s) for s in range(nseq)]
    maths = [gate_math(s, fronts[s][4]) for s in range(nseq)]
    for h in range(H):
        for s in range(nseq):
            xc, qs, ks, vs, _ = fronts[s]
            head(s, h, xc, qs[h], ks[h], vs[h], *maths[s])


def _mlstm(xm, o_pre, conv_w, conv_b, wq, wk, wv, wif, bif, ml_skip, g_mlnorm,
           *, L=256, nseq=2):
    B, S, D = xm.shape
    H, Dh = ML_HEADS, ML_HEAD_DIM
    row = pl.BlockSpec((nseq, L, D), lambda b, i: (b, i, 0))
    return pl.pallas_call(
        functools.partial(_mlstm_kernel, L=L, nseq=nseq),
        out_shape=jax.ShapeDtypeStruct((B, S, D), BF16),
        grid=(B // nseq, S // L),
        in_specs=[row, row, _resident((CONV_WIDTH, D)), _resident((1, D)),
                  _resident((H, Dh, Dh)), _resident((H, Dh, Dh)), _resident((H, Dh, Dh)),
                  _resident((3, H, Dh, 2 * V7X_LANES)), _resident((1, 2 * V7X_LANES)),
                  _resident((1, D)), _resident((1, D))],
        out_specs=row,
        scratch_shapes=[pltpu.VMEM((nseq, V7X_SUBLANES + L, D), F32),
                        pltpu.VMEM((nseq, H, Dh, Dh), F32),
                        pltpu.VMEM((nseq, H, 1, Dh), F32),
                        pltpu.VMEM((nseq, 1, V7X_LANES), F32)],
        compiler_params=_params("parallel", "arbitrary"),
        name="mlstm",
    )(xm, o_pre, conv_w, conv_b.reshape(1, D), wq, wk, wv, wif, bif,
      ml_skip.reshape(1, D), g_mlnorm.reshape(1, D))


def _merge_ffn_kernel(h_ref, gt2_ref, ya_ref, yb_ref, ga_ref, gb_ref,
                      wa_ref, wb_ref, wo_ref, sh_ref, sc_ref, gt_ref, g_ref,
                      wg_ref, wu_ref, wd_ref, gfin_ref, o_ref):
    pa = jnp.dot(ya_ref[...], wa_ref[...], preferred_element_type=F32)
    pb = jnp.dot(yb_ref[...], wb_ref[...], preferred_element_type=F32)
    merged = (_sigmoid(ga_ref[...].astype(F32)) * pa
              + _sigmoid(gb_ref[...].astype(F32)) * pb)
    y = jnp.dot(merged.astype(BF16), wo_ref[...], preferred_element_type=F32)
    x = h_ref[...] + gt2_ref[...] * y
    out = _swiglu_residual(x, sh_ref, sc_ref, gt_ref, g_ref, wg_ref, wu_ref, wd_ref)
    ms = jnp.mean(out * out, axis=-1, keepdims=True)
    o_ref[...] = out * lax.rsqrt(ms + EPS) * gfin_ref[...]


def _merge_ffn(h, mods, ya, yb, ga, gb, wa, wb, wo, g, wg, wu, wd, g_final, *, tm=512):
    B, S, D = h.shape
    row = pl.BlockSpec((None, tm, D), lambda b, i: (b, i, 0))
    return pl.pallas_call(
        _merge_ffn_kernel,
        out_shape=jax.ShapeDtypeStruct((B, S, D), F32),
        grid=(B, S // tm),
        in_specs=[row, _mod_spec(5), row, row, row, row,
                  _resident((D, D)), _resident((D, D)), _resident((D, D)),
                  _mod_spec(6), _mod_spec(7), _mod_spec(8)]
        + _ffn_weight_specs() + [_resident((1, D))],
        out_specs=row,
        compiler_params=_params("parallel", "parallel"),
        name="merge_ffn",
    )(h, mods, ya, yb, ga, gb, wa, wb, wo, mods, mods, mods,
      g.reshape(1, D), wg, wu, wd, g_final.reshape(1, D))


def _rope_tables(S):
    inv = ROPE_THETA ** (-jnp.arange(0, DA_HEAD_DIM, 2, dtype=F32) / DA_HEAD_DIM)
    ang = jnp.arange(S, dtype=F32)[:, None] * inv[None, :]
    cos, sin = jnp.cos(ang), jnp.sin(ang)
    cos_g = jnp.concatenate([cos, cos], axis=-1)
    sin_g = jnp.concatenate([-sin, sin], axis=-1)
    reps = V7X_LANES // DA_HEAD_DIM
    return (jnp.tile(cos_g, (1, reps)), jnp.tile(sin_g, (1, reps)), cos_g.T, sin_g.T)


def _gate_weights(w_if, b_if):
    H = ML_HEADS
    pad = V7X_LANES - H
    wi = jnp.pad(w_if[..., :H], ((0, 0),) * 3 + ((0, pad),))
    wf = jnp.pad(w_if[..., H:], ((0, 0),) * 3 + ((0, pad),))
    bi = jnp.pad(b_if[:H], (0, pad))
    bf = jnp.pad(b_if[H:], (0, pad))
    return (jnp.concatenate([wi, wf], axis=-1).astype(BF16),
            jnp.concatenate([bi, bf]).reshape(1, 2 * V7X_LANES))


def kernel(x, c, w_ada, b_ada, g_ff1, w1_gate, w1_up, w1_down, g_mix, w_in,
           lambda_q1, lambda_k1, lambda_q2, lambda_k2, g_subln, conv_w, conv_b,
           w_mq, w_mk, w_mv, w_if, b_if, ml_skip, g_mlnorm, w_proj_a, w_proj_b,
           w_out, g_ff2, w2_gate, w2_up, w2_down, g_final):
    B, S, D = x.shape
    l = 0
    mods = _ada(c, w_ada[l], b_ada[l]).reshape(N_MOD, B, 1, D)
    tables = _rope_tables(S)
    wif, bif = _gate_weights(w_if[l], b_if[l])
    w_in_bf = w_in[l].astype(BF16)
    wqt = w_in_bf[:, :D].T
    wk = w_in_bf[:, D:2 * D]
    wvt = w_in_bf[:, 2 * D:3 * D].T
    wr = w_in_bf[:, 3 * D:]

    h = _ffn(x, mods, 0, g_ff1[l], w1_gate[l].astype(BF16), w1_up[l].astype(BF16),
             w1_down[l].astype(BF16))
    qt, k1, k2, vt, xm, o_pre, ga, gb = _proj(h, mods, g_mix[l], wqt, wk, wvt, wr, tables,
                                              tm=ATTN_BLOCK)
    ya = _attn(qt, k1, k2, vt, lambda_q1[l].reshape(1, -1), lambda_k1[l].reshape(1, -1),
               lambda_q2[l].reshape(1, -1), lambda_k2[l].reshape(1, -1),
               g_subln[l], tb=ATTN_BLOCK)
    yb = _mlstm(xm, o_pre, conv_w[l], conv_b[l], w_mq[l].astype(BF16),
                w_mk[l].astype(BF16), w_mv[l].astype(BF16), wif, bif,
                ml_skip[l], g_mlnorm[l])
    return _merge_ffn(h, mods, ya, yb, ga, gb, w_proj_a[l].astype(BF16),
                      w_proj_b[l].astype(BF16), w_out[l].astype(BF16), g_ff2[l],
                      w2_gate[l].astype(BF16), w2_up[l].astype(BF16),
                      w2_down[l].astype(BF16), g_final)
```

```python
import functools
import math

import jax
import jax.numpy as jnp
from jax import lax
from jax.experimental import pallas as pl
from jax.experimental.pallas import tpu as pltpu

D_MODEL = 1024
CHUNK = 64
EPS = 1e-6
DA_HEADS = 8
DA_HEAD_DIM = 64
ROPE_THETA = 10000.0
ML_HEADS = 4
ML_HEAD_DIM = 256
CONV_WIDTH = 4
D_FF = 2816
FFN_RES = 0.5
N_MOD = 9
LAMBDA_INIT = 0.8 - 0.6 * math.exp(-0.3 * 0)

V7X_LANES = 128
V7X_SUBLANES = 8
V7X_MXU_COLS = 256
V7X_VMEM_LIMIT_BYTES = 56 * 1024 * 1024

ATTN_BLOCK = 512
VT_ROWS = 128 + 16

BF16 = jnp.bfloat16
F32 = jnp.float32


def _resident(shape):
    nd = len(shape)
    return pl.BlockSpec(shape, lambda *_: (0,) * nd, pipeline_mode=pl.Buffered(1))


def _params(*sem):
    return pltpu.CompilerParams(dimension_semantics=sem,
                                vmem_limit_bytes=V7X_VMEM_LIMIT_BYTES)


def _mod_spec(k):
    return pl.BlockSpec((None, None, 1, D_MODEL), lambda b, i: (k, b, 0, 0))


def _rms_mod(x, g, shift, scale):
    ms = jnp.mean(x * x, axis=-1, keepdims=True)
    xn = x * lax.rsqrt(ms + EPS) * g
    return xn * (1.0 + scale) + shift


def _sigmoid(x):
    return 1.0 / (1.0 + jnp.exp(-x))


def _ada_kernel(c_ref, w_ref, b_ref, o_ref):
    c = c_ref[...]
    a = c * _sigmoid(c)
    o_ref[...] = jnp.dot(a, w_ref[...], preferred_element_type=F32) + b_ref[...]


def _ada(c, w_ada, b_ada):
    B = c.shape[0]
    return pl.pallas_call(
        _ada_kernel,
        out_shape=jax.ShapeDtypeStruct((N_MOD, B, D_MODEL), F32),
        grid=(N_MOD,),
        in_specs=[
            pl.BlockSpec((B, D_MODEL), lambda j: (0, 0)),
            pl.BlockSpec((D_MODEL, D_MODEL), lambda j: (0, j)),
            pl.BlockSpec((None, 1, D_MODEL), lambda j: (j, 0, 0)),
        ],
        out_specs=pl.BlockSpec((None, B, D_MODEL), lambda j: (j, 0, 0)),
        compiler_params=_params("arbitrary"),
        name="ada",
    )(c, w_ada, b_ada.reshape(N_MOD, 1, D_MODEL))


def _swiglu_residual(x, sh_ref, sc_ref, gt_ref, g_ref, wg_ref, wu_ref, wd_ref):
    u = _rms_mod(x, g_ref[...], sh_ref[...], sc_ref[...]).astype(BF16)
    gate = jnp.dot(u, wg_ref[...], preferred_element_type=F32)
    up = jnp.dot(u, wu_ref[...], preferred_element_type=F32)
    a = (gate * _sigmoid(gate) * up).astype(BF16)
    y = jnp.dot(a, wd_ref[...], preferred_element_type=F32)
    return x + FFN_RES * gt_ref[...] * y


def _ffn_kernel(h_ref, sh_ref, sc_ref, gt_ref, g_ref, wg_ref, wu_ref, wd_ref, o_ref):
    o_ref[...] = _swiglu_residual(h_ref[...], sh_ref, sc_ref, gt_ref, g_ref,
                                  wg_ref, wu_ref, wd_ref)


def _ffn_weight_specs():
    return [_resident((1, D_MODEL)), _resident((D_MODEL, D_FF)),
            _resident((D_MODEL, D_FF)), _resident((D_FF, D_MODEL))]


def _ffn(h, mods, k0, g, wg, wu, wd, *, tm=512):
    B, S, D = h.shape
    row = pl.BlockSpec((None, tm, D), lambda b, i: (b, i, 0))
    return pl.pallas_call(
        _ffn_kernel,
        out_shape=jax.ShapeDtypeStruct((B, S, D), F32),
        grid=(B, S // tm),
        in_specs=[row, _mod_spec(k0), _mod_spec(k0 + 1), _mod_spec(k0 + 2)]
        + _ffn_weight_specs(),
        out_specs=row,
        compiler_params=_params("parallel", "parallel"),
        name="ffn",
    )(h, mods, mods, mods, g.reshape(1, D), wg, wu, wd)


def _rope_rows(t, cos, sin_signed):
    half = DA_HEAD_DIM // 2
    partner = jnp.concatenate([t[half:], t[:half]], axis=0)
    return t * cos + partner * sin_signed


def _rope_lanes(t, cos, sin_signed, first_half):
    outs = []
    for gidx in range(t.shape[1] // V7X_LANES):
        x = t[:, gidx * V7X_LANES:(gidx + 1) * V7X_LANES]
        partner = jnp.where(first_half,
                            pltpu.roll(x, V7X_LANES - DA_HEAD_DIM // 2, 1),
                            pltpu.roll(x, DA_HEAD_DIM // 2, 1))
        outs.append(x * cos + partner * sin_signed)
    return outs


def _proj_kernel(h_ref, sh_ref, sc_ref, g_ref, wqt_ref, wk_ref, wvt_ref, wr_ref,
                 cos_ref, sin_ref, cost_ref, sint_ref,
                 qt_ref, k1_ref, k2_ref, vt_ref, xm_ref, op_ref, ga_ref, gb_ref):
    x = h_ref[...]
    u = _rms_mod(x, g_ref[...], sh_ref[...], sc_ref[...]).astype(BF16)
    nt = (((1,), (1,)), ((), ()))

    q_scale = DA_HEAD_DIM ** -0.5 * math.log2(math.e)
    qt = lax.dot_general(wqt_ref[...], u, nt, preferred_element_type=F32)
    cost = cost_ref[...]
    sint = sint_ref[...]
    for gidx in range(D_MODEL // DA_HEAD_DIM):
        rows = slice(gidx * DA_HEAD_DIM, (gidx + 1) * DA_HEAD_DIM)
        qt_ref[rows, :] = (_rope_rows(qt[rows], cost, sint) * q_scale).astype(BF16)

    lane = lax.broadcasted_iota(jnp.int32, (x.shape[0], V7X_LANES), 1)
    first_half = (lane % DA_HEAD_DIM) < (DA_HEAD_DIM // 2)
    comp0 = lane < DA_HEAD_DIM
    k = jnp.dot(u, wk_ref[...], preferred_element_type=F32)
    for gidx, t in enumerate(_rope_lanes(k, cos_ref[...], sin_ref[...], first_half)):
        sl = slice(gidx * V7X_LANES, (gidx + 1) * V7X_LANES)
        k1_ref[:, sl] = jnp.where(comp0, t, 0.0).astype(BF16)
        k2_ref[:, sl] = jnp.where(comp0, 0.0, t).astype(BF16)

    vt = lax.dot_general(wvt_ref[...], u, nt, preferred_element_type=F32).astype(BF16)
    extra = VT_ROWS - V7X_LANES
    ones_rows = (lax.broadcasted_iota(jnp.int32, (extra, x.shape[0]), 0) == 0).astype(BF16)
    for hd in range(DA_HEADS):
        vt_ref[hd * VT_ROWS:hd * VT_ROWS + V7X_LANES, :] = (
            vt[hd * V7X_LANES:(hd + 1) * V7X_LANES])
        vt_ref[hd * VT_ROWS + V7X_LANES:(hd + 1) * VT_ROWS, :] = ones_rows
    for s, ref in enumerate((xm_ref, op_ref, ga_ref, gb_ref)):
        ref[...] = jnp.dot(u, wr_ref[:, s * D_MODEL:(s + 1) * D_MODEL],
                           preferred_element_type=F32).astype(ref.dtype)


def _proj(h, mods, g, wqt, wk, wvt, wr, tables, *, tm):
    B, S, D = h.shape
    cos, sin, cost, sint = tables
    row = pl.BlockSpec((None, tm, D), lambda b, i: (b, i, 0))
    col = pl.BlockSpec((None, None, D, tm), lambda b, i: (b, i, 0, 0))
    tab = pl.BlockSpec((tm, V7X_LANES), lambda b, i: (i, 0))
    tabt = pl.BlockSpec((DA_HEAD_DIM, tm), lambda b, i: (0, i))
    bf = jax.ShapeDtypeStruct((B, S, D), BF16)
    bft = jax.ShapeDtypeStruct((B, S // tm, D, tm), BF16)
    vrows = DA_HEADS * VT_ROWS
    vspec = pl.BlockSpec((None, None, vrows, tm), lambda b, i: (b, i, 0, 0))
    bfv = jax.ShapeDtypeStruct((B, S // tm, vrows, tm), BF16)
    f32 = jax.ShapeDtypeStruct((B, S, D), F32)
    return pl.pallas_call(
        _proj_kernel,
        out_shape=(bft, bf, bf, bfv, f32, bf, bf, bf),
        grid=(B, S // tm),
        in_specs=[row, _mod_spec(3), _mod_spec(4), _resident((1, D)),
                  _resident((D, D)), _resident((D, D)), _resident((D, D)),
                  _resident((D, 4 * D)), tab, tab, tabt, tabt],
        out_specs=(col, row, row, vspec, row, row, row, row),
        compiler_params=_params("parallel", "parallel"),
        name="mixer_proj",
    )(h, mods, mods, g.reshape(1, D), wqt, wk, wvt, wr, cos, sin, cost, sint)


def _attn_kernel(lq1_ref, lk1_ref, lq2_ref, lk2_ref, gsub_ref,
                 qt_ref, k1_ref, k2_ref, vt_ref, o_ref, s0_ref, s1_ref, s2_ref,
                 m_ref, l_ref, acc_ref, *, tb):
    t = pl.program_id(2)
    wide = 2 * tb
    lam = (jnp.exp(jnp.sum(lq1_ref[...] * lk1_ref[...], axis=-1, keepdims=True))
           - jnp.exp(jnp.sum(lq2_ref[...] * lk2_ref[...], axis=-1, keepdims=True))
           + LAMBDA_INIT)

    def head_ops(hd):
        lanes = slice(hd * V7X_LANES, (hd + 1) * V7X_LANES)
        qt = jnp.concatenate([qt_ref[0, lanes, :], qt_ref[1, lanes, :]],
                             axis=1)
        k_refs = (k1_ref, k2_ref)

        def scores(c, j, s_ref, c0=0):
            start = pl.multiple_of(j * tb, tb)
            s_ref[c, :, c0:] = jnp.dot(k_refs[c][pl.ds(start, tb), lanes], qt[:, c0:],
                                       preferred_element_type=F32)

        def softmax_pv(c, j, s_ref, c0=0, key_off=None):
            vt = vt_ref[j, hd * VT_ROWS:(hd + 1) * VT_ROWS, :]
            s = s_ref[c, :, c0:]
            if key_off is not None:
                shape = (tb, wide - c0)
                kc = (lax.broadcasted_iota(jnp.int32, shape, 0) + key_off) // CHUNK
                qc = (lax.broadcasted_iota(jnp.int32, shape, 1) + c0) // CHUNK
                s = jnp.where(kc <= qc, s, -jnp.inf)
            m = m_ref[hd, c, :, c0:]
            m_new = jnp.maximum(m, jnp.max(s, axis=0, keepdims=True))
            alpha = jnp.exp2(m - m_new)
            p = jnp.exp2(s - m_new)
            pv = jnp.dot(vt, p.astype(BF16), preferred_element_type=F32)
            m_ref[hd, c, :, c0:] = m_new
            l_ref[hd, c, :, c0:] = (alpha * l_ref[hd, c, :, c0:]
                                    + pv[V7X_LANES:V7X_LANES + 1])
            acc_ref[hd, c, :, c0:] = alpha * acc_ref[hd, c, :, c0:] + pv[:V7X_LANES]

        def first_scores(slot):
            m_ref[hd] = jnp.full(m_ref.shape[1:], -jnp.inf, F32)
            l_ref[hd] = jnp.zeros(l_ref.shape[1:], F32)
            acc_ref[hd] = jnp.zeros(acc_ref.shape[1:], F32)
            for c in range(2):
                scores(c, 0, slot)

        def full_blocks(slot_a, slot_b):
            @pl.loop(0, t)
            def _(u):
                for cur, nxt, j in ((slot_a, slot_b, 2 * u), (slot_b, slot_a, 2 * u + 1)):
                    for c in range(2):
                        scores(c, j + 1, nxt)
                        softmax_pv(c, j, cur)

        def diagonal(slot_a, slot_b, between=None):
            for c in range(2):
                scores(c, 2 * t + 1, slot_b, c0=tb)
                softmax_pv(c, 2 * t, slot_a, key_off=0)
            if between is not None:
                between()
            for c in range(2):
                softmax_pv(c, 2 * t + 1, slot_b, c0=tb, key_off=tb)
            o = (acc_ref[hd, 0] * (1.0 / l_ref[hd, 0])
                 - acc_ref[hd, 1] * (lam / l_ref[hd, 1]))
            ms = jnp.mean(o * o, axis=0, keepdims=True)
            o = o * lax.rsqrt(ms + EPS) * (gsub_ref[...] * (1.0 - LAMBDA_INIT))
            o_ref[:, lanes] = o.T.astype(o_ref.dtype)

        return first_scores, full_blocks, diagonal

    first0, full0, diag0 = head_ops(0)
    first1, full1, diag1 = head_ops(1)
    first0(s0_ref)
    full0(s0_ref, s1_ref)
    diag0(s0_ref, s1_ref, between=lambda: first1(s2_ref))
    full1(s2_ref, s1_ref)
    diag1(s2_ref, s1_ref)


def _attn(qt, k1, k2, vt, lq1, lk1, lq2, lk2, g_subln, *, tb):
    B, S, D = k1.shape
    nb = S // tb
    hp = 2
    lam_spec = pl.BlockSpec((1, DA_HEAD_DIM), lambda b, h, i: (0, 0))
    kspec = pl.BlockSpec((None, S, hp * V7X_LANES), lambda b, h, i: (b, 0, h))
    steps = nb // 2
    stat = pltpu.VMEM((hp, 2, 1, 2 * tb), F32)
    return pl.pallas_call(
        functools.partial(_attn_kernel, tb=tb),
        out_shape=jax.ShapeDtypeStruct((B, S, D), BF16),
        grid=(B, DA_HEADS // hp, steps),
        in_specs=[lam_spec] * 4 + [
            pl.BlockSpec((V7X_LANES, 1), lambda b, h, i: (0, 0)),
            pl.BlockSpec((None, 2, hp * V7X_LANES, tb), lambda b, h, i: (b, i, h, 0)),
            kspec, kspec,
            pl.BlockSpec((None, nb, hp * VT_ROWS, tb), lambda b, h, i: (b, 0, h, 0))],
        out_specs=pl.BlockSpec((None, 2 * tb, hp * V7X_LANES), lambda b, h, i: (b, i, h)),
        scratch_shapes=[pltpu.VMEM((2, tb, 2 * tb), F32)] * 3 + [
            stat, stat, pltpu.VMEM((hp, 2, V7X_LANES, 2 * tb), F32)],
        compiler_params=_params("parallel", "parallel", "arbitrary"),
        name="diff_attn",
    )(lq1, lk1, lq2, lk2, g_subln.reshape(V7X_LANES, 1), qt, k1, k2, vt)


def _scan_rows(x, op, identity):
    n = x.shape[0]
    row = lax.broadcasted_iota(jnp.int32, x.shape, 0)
    shift = 1
    while shift < n:
        x = op(x, jnp.where(row >= shift, pltpu.roll(x, shift, 0), identity))
        shift *= 2
    return x


def _lane(x, h):
    return x[:, h:h + 1]


def _mlstm_kernel(xm_ref, op_ref, cw_ref, cb_ref, wq_ref, wk_ref, wv_ref,
                  wif_ref, bif_ref, skip_ref, gn_ref, yb_ref,
                  xbuf, c_st, n_st, m_st, *, L, nseq):
    H, Dh = ML_HEADS, ML_HEAD_DIM
    halo = V7X_SUBLANES

    @pl.when(pl.program_id(1) == 0)
    def _():
        xbuf[:, 0:halo, :] = jnp.zeros((nseq, halo, H * Dh), F32)
        c_st[...] = jnp.zeros_like(c_st)
        n_st[...] = jnp.zeros_like(n_st)
        m_st[...] = jnp.zeros_like(m_st)

    def front(s):
        xm = xm_ref[s]
        xbuf[s, halo:halo + L, :] = xm
        conv = cb_ref[...] + cw_ref[CONV_WIDTH - 1:CONV_WIDTH, :] * xm
        for d in range(1, CONV_WIDTH):
            conv = conv + (cw_ref[CONV_WIDTH - 1 - d:CONV_WIDTH - d, :]
                           * xbuf[s, halo - d:halo - d + L, :])
        xbuf[s, 0:halo, :] = xm[L - halo:L, :]
        xc = conv * _sigmoid(conv)
        qs, ks, vs = [], [], []
        gates = jnp.broadcast_to(bif_ref[...], (2 * V7X_SUBLANES, L))
        for h in range(H):
            sl = slice(h * Dh, (h + 1) * Dh)
            xc_h = xc[:, sl].astype(BF16)
            q = jnp.dot(xc_h, wq_ref[h], preferred_element_type=F32)
            k = jnp.dot(xc_h, wk_ref[h], preferred_element_type=F32)
            v = jnp.dot(xm[:, sl].astype(BF16), wv_ref[h], preferred_element_type=F32)
            for j, t in enumerate((q, k, v)):
                gates = gates + lax.dot_general(wif_ref[j, h], t.astype(BF16), nt,
                                                preferred_element_type=F32)
            qs.append(q)
            ks.append(k)
            vs.append(v)
        return xc, qs, ks, vs, gates

    def gate_math(s, gates):
        pad = jnp.zeros((V7X_LANES - V7X_SUBLANES, L), F32)

        def columns(rows):
            return jnp.concatenate([rows, pad], axis=0).T

        gi = columns(gates[:V7X_SUBLANES])
        b = _scan_rows(columns(jax.nn.log_sigmoid(gates[V7X_SUBLANES:])), jnp.add, 0.0)
        a = gi - b
        m_prev = m_st[s]
        mx = jnp.maximum(m_prev, _scan_rows(a, jnp.maximum, -jnp.inf))
        b_last = b[L - 1:L, :]
        m_new = b_last + mx[L - 1:L, :]
        decay = jnp.exp(b_last + m_prev - m_new)
        wg = jnp.exp(b_last + a - m_new)
        m_st[s] = m_new
        return (-mx, jnp.exp(m_prev - mx), wg, jnp.exp(-(b + mx))), a.T, decay

    tril = (lax.broadcasted_iota(jnp.int32, (L, L), 1)
            <= lax.broadcasted_iota(jnp.int32, (L, L), 0))
    nt = (((1,), (1,)), ((), ()))
    tn = (((0,), (0,)), ((), ()))

    def head(s, h, xc, q, k, v, cols, a, decay):
        neg_mx, w_carry, w_state, exp_neg_m = cols
        sl = slice(h * Dh, (h + 1) * Dh)
        qb, kb, vb = q.astype(BF16), k.astype(BF16), v.astype(BF16)
        w = jnp.exp(jnp.where(tril, _lane(neg_mx, h) + a[h:h + 1, :], -jnp.inf))
        sc = lax.dot_general(qb, kb, nt, preferred_element_type=F32) * w
        w_inter = _lane(w_carry, h)
        c_prev = c_st[s, h]
        n_prev = n_st[s, h]
        num = (jnp.dot(sc.astype(BF16), vb, preferred_element_type=F32)
               + w_inter * jnp.dot(qb, c_prev.astype(BF16), preferred_element_type=F32))
        den = (jnp.sum(sc, axis=-1, keepdims=True)
               + w_inter * jnp.sum(q * n_prev, axis=-1, keepdims=True))
        hcell = num / jnp.maximum(jnp.abs(den), _lane(exp_neg_m, h))

        wk = _lane(w_state, h) * k
        dec = _lane(decay, h)
        c_st[s, h] = dec * c_prev + lax.dot_general(wk.astype(BF16), vb, tn,
                                                    preferred_element_type=F32)
        n_st[s, h] = dec * n_prev + jnp.sum(wk, axis=0, keepdims=True)

        ms = jnp.mean(hcell * hcell, axis=-1, keepdims=True)
        hn = hcell * lax.rsqrt(ms + EPS) * gn_ref[:, sl]
        y = ((hn + skip_ref[:, sl] * xc[:, sl])
             * _sigmoid(op_ref[s, :, sl].astype(F32)))
        yb_ref[s, :, sl] = y.astype(yb_ref.dtype)

    fronts, maths = [], []
    for s in range(nseq):
        fronts.append(front(s))
        maths.append(gate_math(s, fronts[s][4]))
    for h in range(H):
        for s in range(nseq):
            xc, qs, ks, vs, _ = fronts[s]
            head(s, h, xc, qs[h], ks[h], vs[h], *maths[s])


def _mlstm(xm, o_pre, conv_w, conv_b, wq, wk, wv, wif, bif, ml_skip, g_mlnorm,
           *, L=256, nseq=2):
    B, S, D = xm.shape
    H, Dh = ML_HEADS, ML_HEAD_DIM
    row = pl.BlockSpec((nseq, L, D), lambda b, i: (b, i, 0))
    return pl.pallas_call(
        functools.partial(_mlstm_kernel, L=L, nseq=nseq),
        out_shape=jax.ShapeDtypeStruct((B, S, D), BF16),
        grid=(B // nseq, S // L),
        in_specs=[row, row, _resident((CONV_WIDTH, D)), _resident((1, D)),
                  _resident((H, Dh, Dh)), _resident((H, Dh, Dh)), _resident((H, Dh, Dh)),
                  _resident((3, H, 2 * V7X_SUBLANES, Dh)), _resident((2 * V7X_SUBLANES, 1)),
                  _resident((1, D)), _resident((1, D))],
        out_specs=row,
        scratch_shapes=[pltpu.VMEM((nseq, V7X_SUBLANES + L, D), F32),
                        pltpu.VMEM((nseq, H, Dh, Dh), F32),
                        pltpu.VMEM((nseq, H, 1, Dh), F32),
                        pltpu.VMEM((nseq, 1, V7X_LANES), F32)],
        compiler_params=_params("parallel", "arbitrary"),
        name="mlstm",
    )(xm, o_pre, conv_w, conv_b.reshape(1, D), wq, wk, wv, wif, bif,
      ml_skip.reshape(1, D), g_mlnorm.reshape(1, D))


def _merge_ffn_kernel(h_ref, gt2_ref, ya_ref, yb_ref, ga_ref, gb_ref,
                      wa_ref, wb_ref, wo_ref, sh_ref, sc_ref, gt_ref, g_ref,
                      wg_ref, wu_ref, wd_ref, gfin_ref, o_ref):
    pa = jnp.dot(ya_ref[...], wa_ref[...], preferred_element_type=F32)
    pb = jnp.dot(yb_ref[...], wb_ref[...], preferred_element_type=F32)
    merged = (_sigmoid(ga_ref[...].astype(F32)) * pa
              + _sigmoid(gb_ref[...].astype(F32)) * pb)
    y = jnp.dot(merged.astype(BF16), wo_ref[...], preferred_element_type=F32)
    x = h_ref[...] + gt2_ref[...] * y
    out = _swiglu_residual(x, sh_ref, sc_ref, gt_ref, g_ref, wg_ref, wu_ref, wd_ref)
    ms = jnp.mean(out * out, axis=-1, keepdims=True)
    o_ref[...] = out * lax.rsqrt(ms + EPS) * gfin_ref[...]


def _merge_ffn(h, mods, ya, yb, ga, gb, wa, wb, wo, g, wg, wu, wd, g_final, *, tm=512):
    B, S, D = h.shape
    row = pl.BlockSpec((None, tm, D), lambda b, i: (b, i, 0))
    return pl.pallas_call(
        _merge_ffn_kernel,
        out_shape=jax.ShapeDtypeStruct((B, S, D), F32),
        grid=(B, S // tm),
        in_specs=[row, _mod_spec(5), row, row, row, row,
                  _resident((D, D)), _resident((D, D)), _resident((D, D)),
                  _mod_spec(6), _mod_spec(7), _mod_spec(8)]
        + _ffn_weight_specs() + [_resident((1, D))],
        out_specs=row,
        compiler_params=_params("parallel", "parallel"),
        name="merge_ffn",
    )(h, mods, ya, yb, ga, gb, wa, wb, wo, mods, mods, mods,
      g.reshape(1, D), wg, wu, wd, g_final.reshape(1, D))


def _rope_tables(S):
    inv = ROPE_THETA ** (-jnp.arange(0, DA_HEAD_DIM, 2, dtype=F32) / DA_HEAD_DIM)
    ang = jnp.arange(S, dtype=F32)[:, None] * inv[None, :]
    cos, sin = jnp.cos(ang), jnp.sin(ang)
    cos_g = jnp.concatenate([cos, cos], axis=-1)
    sin_g = jnp.concatenate([-sin, sin], axis=-1)
    reps = V7X_LANES // DA_HEAD_DIM
    return (jnp.tile(cos_g, (1, reps)), jnp.tile(sin_g, (1, reps)), cos_g.T, sin_g.T)


def _gate_weights(w_if, b_if):
    H = ML_HEADS
    pad = V7X_SUBLANES - H
    wt = jnp.swapaxes(w_if, -1, -2)
    wt = wt * jnp.array([1.0, ML_HEAD_DIM ** 0.5, 1.0], F32)[:, None, None, None]
    rows = ((0, 0),) * 2 + ((0, pad), (0, 0))
    wt = jnp.concatenate([jnp.pad(wt[:, :, :H], rows), jnp.pad(wt[:, :, H:], rows)], axis=2)
    bias = jnp.concatenate([jnp.pad(b_if[:H], (0, pad)), jnp.pad(b_if[H:], (0, pad))])
    return wt.astype(BF16), bias.reshape(2 * V7X_SUBLANES, 1)


def kernel(x, c, w_ada, b_ada, g_ff1, w1_gate, w1_up, w1_down, g_mix, w_in,
           lambda_q1, lambda_k1, lambda_q2, lambda_k2, g_subln, conv_w, conv_b,
           w_mq, w_mk, w_mv, w_if, b_if, ml_skip, g_mlnorm, w_proj_a, w_proj_b,
           w_out, g_ff2, w2_gate, w2_up, w2_down, g_final):
    B, S, D = x.shape
    l = 0
    mods = _ada(c, w_ada[l], b_ada[l]).reshape(N_MOD, B, 1, D)
    tables = _rope_tables(S)
    wif, bif = _gate_weights(w_if[l], b_if[l])
    w_in_bf = w_in[l].astype(BF16)
    wqt = w_in_bf[:, :D].T
    wk = w_in_bf[:, D:2 * D]
    wvt = w_in_bf[:, 2 * D:3 * D].T
    wr = w_in_bf[:, 3 * D:]

    h = _ffn(x, mods, 0, g_ff1[l], w1_gate[l].astype(BF16), w1_up[l].astype(BF16),
             w1_down[l].astype(BF16))
    qt, k1, k2, vt, xm, o_pre, ga, gb = _proj(h, mods, g_mix[l], wqt, wk, wvt, wr, tables,
                                              tm=ATTN_BLOCK)
    ya = _attn(qt, k1, k2, vt, lambda_q1[l].reshape(1, -1), lambda_k1[l].reshape(1, -1),
               lambda_q2[l].reshape(1, -1), lambda_k2[l].reshape(1, -1),
               g_subln[l], tb=ATTN_BLOCK)
    yb = _mlstm(xm, o_pre, conv_w[l], conv_b[l], w_mq[l].astype(BF16),
                (w_mk[l] * ML_HEAD_DIM ** -0.5).astype(BF16), w_mv[l].astype(BF16), wif, bif,
                ml_skip[l], g_mlnorm[l])
    return _merge_ffn(h, mods, ya, yb, ga, gb, w_proj_a[l].astype(BF16),
                      w_proj_b[l].astype(BF16), w_out[l].astype(BF16), g_ff2[l],
                      w2_gate[l].astype(BF16), w2_up[l].astype(BF16),
                      w2_down[l].astype(BF16), g_final)
```

```python
import functools
import math

import jax
import jax.numpy as jnp
from jax import lax
from jax.experimental import pallas as pl
from jax.experimental.pallas import tpu as pltpu

D_MODEL = 1024
CHUNK = 64
EPS = 1e-6
DA_HEADS = 8
DA_HEAD_DIM = 64
ROPE_THETA = 10000.0
ML_HEADS = 4
ML_HEAD_DIM = 256
CONV_WIDTH = 4
D_FF = 2816
FFN_RES = 0.5
N_MOD = 9
LAMBDA_INIT = 0.8 - 0.6 * math.exp(-0.3 * 0)

V7X_LANES = 128
V7X_SUBLANES = 8
V7X_MXU_COLS = 256
V7X_VMEM_LIMIT_BYTES = 56 * 1024 * 1024

ATTN_BLOCK = 512
VT_ROWS = 128 + 16

BF16 = jnp.bfloat16
F32 = jnp.float32


def _resident(shape):
    nd = len(shape)
    return pl.BlockSpec(shape, lambda *_: (0,) * nd, pipeline_mode=pl.Buffered(1))


def _params(*sem):
    return pltpu.CompilerParams(dimension_semantics=sem,
                                vmem_limit_bytes=V7X_VMEM_LIMIT_BYTES)


def _mod_spec(k):
    return pl.BlockSpec((None, None, 1, D_MODEL), lambda b, i: (k, b, 0, 0))


def _rms_mod(x, g, shift, scale):
    ms = jnp.mean(x * x, axis=-1, keepdims=True)
    xn = x * lax.rsqrt(ms + EPS) * g
    return xn * (1.0 + scale) + shift


def _sigmoid(x):
    return 1.0 / (1.0 + jnp.exp(-x))


def _ada_kernel(c_ref, w_ref, b_ref, o_ref):
    c = c_ref[...]
    a = c * _sigmoid(c)
    o_ref[...] = jnp.dot(a, w_ref[...], preferred_element_type=F32) + b_ref[...]


def _ada(c, w_ada, b_ada):
    B = c.shape[0]
    return pl.pallas_call(
        _ada_kernel,
        out_shape=jax.ShapeDtypeStruct((N_MOD, B, D_MODEL), F32),
        grid=(N_MOD,),
        in_specs=[
            pl.BlockSpec((B, D_MODEL), lambda j: (0, 0)),
            pl.BlockSpec((D_MODEL, D_MODEL), lambda j: (0, j)),
            pl.BlockSpec((None, 1, D_MODEL), lambda j: (j, 0, 0)),
        ],
        out_specs=pl.BlockSpec((None, B, D_MODEL), lambda j: (j, 0, 0)),
        compiler_params=_params("arbitrary"),
        name="ada",
    )(c, w_ada, b_ada.reshape(N_MOD, 1, D_MODEL))


def _swiglu_residual(x, sh_ref, sc_ref, gt_ref, g_ref, wg_ref, wu_ref, wd_ref):
    u = _rms_mod(x, g_ref[...], sh_ref[...], sc_ref[...]).astype(BF16)
    gate = jnp.dot(u, wg_ref[...], preferred_element_type=F32)
    up = jnp.dot(u, wu_ref[...], preferred_element_type=F32)
    a = (gate * _sigmoid(gate) * up).astype(BF16)
    y = jnp.dot(a, wd_ref[...], preferred_element_type=F32)
    return x + FFN_RES * gt_ref[...] * y


def _ffn_kernel(h_ref, sh_ref, sc_ref, gt_ref, g_ref, wg_ref, wu_ref, wd_ref, o_ref):
    o_ref[...] = _swiglu_residual(h_ref[...], sh_ref, sc_ref, gt_ref, g_ref,
                                  wg_ref, wu_ref, wd_ref)


def _ffn_weight_specs():
    return [_resident((1, D_MODEL)), _resident((D_MODEL, D_FF)),
            _resident((D_MODEL, D_FF)), _resident((D_FF, D_MODEL))]


def _ffn(h, mods, k0, g, wg, wu, wd, *, tm=512):
    B, S, D = h.shape
    row = pl.BlockSpec((None, tm, D), lambda b, i: (b, i, 0))
    return pl.pallas_call(
        _ffn_kernel,
        out_shape=jax.ShapeDtypeStruct((B, S, D), F32),
        grid=(B, S // tm),
        in_specs=[row, _mod_spec(k0), _mod_spec(k0 + 1), _mod_spec(k0 + 2)]
        + _ffn_weight_specs(),
        out_specs=row,
        compiler_params=_params("parallel", "parallel"),
        name="ffn",
    )(h, mods, mods, mods, g.reshape(1, D), wg, wu, wd)


def _rope_rows(t, cos, sin_signed):
    half = DA_HEAD_DIM // 2
    partner = jnp.concatenate([t[half:], t[:half]], axis=0)
    return t * cos + partner * sin_signed


def _rope_lanes(t, cos, sin_signed, first_half):
    outs = []
    for gidx in range(t.shape[1] // V7X_LANES):
        x = t[:, gidx * V7X_LANES:(gidx + 1) * V7X_LANES]
        partner = jnp.where(first_half,
                            pltpu.roll(x, V7X_LANES - DA_HEAD_DIM // 2, 1),
                            pltpu.roll(x, DA_HEAD_DIM // 2, 1))
        outs.append(x * cos + partner * sin_signed)
    return outs


def _proj_kernel(h_ref, sh_ref, sc_ref, g_ref, wqt_ref, wk_ref, wvt_ref, wr_ref,
                 cos_ref, sin_ref, cost_ref, sint_ref,
                 qt_ref, k1_ref, k2_ref, vt_ref, xm_ref, op_ref, ga_ref, gb_ref):
    x = h_ref[...]
    u = _rms_mod(x, g_ref[...], sh_ref[...], sc_ref[...]).astype(BF16)
    nt = (((1,), (1,)), ((), ()))

    q_scale = DA_HEAD_DIM ** -0.5 * math.log2(math.e)
    qt = lax.dot_general(wqt_ref[...], u, nt, preferred_element_type=F32)
    cost = cost_ref[...]
    sint = sint_ref[...]
    for gidx in range(D_MODEL // DA_HEAD_DIM):
        rows = slice(gidx * DA_HEAD_DIM, (gidx + 1) * DA_HEAD_DIM)
        qt_ref[rows, :] = (_rope_rows(qt[rows], cost, sint) * q_scale).astype(BF16)

    lane = lax.broadcasted_iota(jnp.int32, (x.shape[0], V7X_LANES), 1)
    first_half = (lane % DA_HEAD_DIM) < (DA_HEAD_DIM // 2)
    comp0 = lane < DA_HEAD_DIM
    k = jnp.dot(u, wk_ref[...], preferred_element_type=F32)
    for gidx, t in enumerate(_rope_lanes(k, cos_ref[...], sin_ref[...], first_half)):
        sl = slice(gidx * V7X_LANES, (gidx + 1) * V7X_LANES)
        k1_ref[:, sl] = jnp.where(comp0, t, 0.0).astype(BF16)
        k2_ref[:, sl] = jnp.where(comp0, 0.0, t).astype(BF16)

    vt = lax.dot_general(wvt_ref[...], u, nt, preferred_element_type=F32).astype(BF16)
    extra = VT_ROWS - V7X_LANES
    ones_rows = (lax.broadcasted_iota(jnp.int32, (extra, x.shape[0]), 0) == 0).astype(BF16)
    for hd in range(DA_HEADS):
        vt_ref[hd * VT_ROWS:hd * VT_ROWS + V7X_LANES, :] = (
            vt[hd * V7X_LANES:(hd + 1) * V7X_LANES])
        vt_ref[hd * VT_ROWS + V7X_LANES:(hd + 1) * VT_ROWS, :] = ones_rows
    for s, ref in enumerate((xm_ref, op_ref, ga_ref, gb_ref)):
        ref[...] = jnp.dot(u, wr_ref[:, s * D_MODEL:(s + 1) * D_MODEL],
                           preferred_element_type=F32).astype(ref.dtype)


def _proj(h, mods, g, wqt, wk, wvt, wr, tables, *, tm):
    B, S, D = h.shape
    cos, sin, cost, sint = tables
    row = pl.BlockSpec((None, tm, D), lambda b, i: (b, i, 0))
    col = pl.BlockSpec((None, None, D, tm), lambda b, i: (b, i, 0, 0))
    tab = pl.BlockSpec((tm, V7X_LANES), lambda b, i: (i, 0))
    tabt = pl.BlockSpec((DA_HEAD_DIM, tm), lambda b, i: (0, i))
    bf = jax.ShapeDtypeStruct((B, S, D), BF16)
    bft = jax.ShapeDtypeStruct((B, S // tm, D, tm), BF16)
    vrows = DA_HEADS * VT_ROWS
    vspec = pl.BlockSpec((None, None, vrows, tm), lambda b, i: (b, i, 0, 0))
    bfv = jax.ShapeDtypeStruct((B, S // tm, vrows, tm), BF16)
    f32 = jax.ShapeDtypeStruct((B, S, D), F32)
    return pl.pallas_call(
        _proj_kernel,
        out_shape=(bft, bf, bf, bfv, f32, bf, bf, bf),
        grid=(B, S // tm),
        in_specs=[row, _mod_spec(3), _mod_spec(4), _resident((1, D)),
                  _resident((D, D)), _resident((D, D)), _resident((D, D)),
                  _resident((D, 4 * D)), tab, tab, tabt, tabt],
        out_specs=(col, row, row, vspec, row, row, row, row),
        compiler_params=_params("parallel", "parallel"),
        name="mixer_proj",
    )(h, mods, mods, g.reshape(1, D), wqt, wk, wvt, wr, cos, sin, cost, sint)


def _attn_kernel(lq1_ref, lk1_ref, lq2_ref, lk2_ref, gsub_ref,
                 qt_ref, k1_ref, k2_ref, vt_ref, o_ref, s0_ref, s1_ref, s2_ref,
                 m_ref, l_ref, acc_ref, *, tb):
    t = pl.program_id(2)
    wide = 2 * tb
    lam = (jnp.exp(jnp.sum(lq1_ref[...] * lk1_ref[...], axis=-1, keepdims=True))
           - jnp.exp(jnp.sum(lq2_ref[...] * lk2_ref[...], axis=-1, keepdims=True))
           + LAMBDA_INIT)

    def head_ops(hd):
        lanes = slice(hd * V7X_LANES, (hd + 1) * V7X_LANES)
        qt = jnp.concatenate([qt_ref[0, lanes, :], qt_ref[1, lanes, :]],
                             axis=1)
        k_refs = (k1_ref, k2_ref)

        def scores(c, j, s_ref, c0=0):
            start = pl.multiple_of(j * tb, tb)
            s_ref[c, :, c0:] = jnp.dot(k_refs[c][pl.ds(start, tb), lanes], qt[:, c0:],
                                       preferred_element_type=F32)

        def softmax_pv(c, j, s_ref, c0=0, key_off=None):
            vt = vt_ref[j, hd * VT_ROWS:(hd + 1) * VT_ROWS, :]
            s = s_ref[c, :, c0:]
            if key_off is not None:
                shape = (tb, wide - c0)
                kc = (lax.broadcasted_iota(jnp.int32, shape, 0) + key_off) // CHUNK
                qc = (lax.broadcasted_iota(jnp.int32, shape, 1) + c0) // CHUNK
                s = jnp.where(kc <= qc, s, -jnp.inf)
            m = m_ref[hd, c, :, c0:]
            m_new = jnp.maximum(m, jnp.max(s, axis=0, keepdims=True))
            alpha = jnp.exp2(m - m_new)
            p = jnp.exp2(s - m_new)
            pv = jnp.dot(vt, p.astype(BF16), preferred_element_type=F32)
            m_ref[hd, c, :, c0:] = m_new
            l_ref[hd, c, :, c0:] = (alpha * l_ref[hd, c, :, c0:]
                                    + pv[V7X_LANES:V7X_LANES + 1])
            acc_ref[hd, c, :, c0:] = alpha * acc_ref[hd, c, :, c0:] + pv[:V7X_LANES]

        def first_scores(slot):
            m_ref[hd] = jnp.full(m_ref.shape[1:], -jnp.inf, F32)
            l_ref[hd] = jnp.zeros(l_ref.shape[1:], F32)
            acc_ref[hd] = jnp.zeros(acc_ref.shape[1:], F32)
            for c in range(2):
                scores(c, 0, slot)

        def full_blocks(slot_a, slot_b):
            @pl.loop(0, t)
            def _(u):
                for cur, nxt, j in ((slot_a, slot_b, 2 * u), (slot_b, slot_a, 2 * u + 1)):
                    for c in range(2):
                        scores(c, j + 1, nxt)
                        softmax_pv(c, j, cur)

        def diagonal(slot_a, slot_b, between=None):
            for c in range(2):
                scores(c, 2 * t + 1, slot_b, c0=tb)
                softmax_pv(c, 2 * t, slot_a, key_off=0)
            if between is not None:
                between()
            for c in range(2):
                softmax_pv(c, 2 * t + 1, slot_b, c0=tb, key_off=tb)
            o = (acc_ref[hd, 0] * (1.0 / l_ref[hd, 0])
                 - acc_ref[hd, 1] * (lam / l_ref[hd, 1]))
            ms = jnp.mean(o * o, axis=0, keepdims=True)
            o = o * lax.rsqrt(ms + EPS) * (gsub_ref[...] * (1.0 - LAMBDA_INIT))
            o_ref[:, lanes] = o.T.astype(o_ref.dtype)

        return first_scores, full_blocks, diagonal

    first0, full0, diag0 = head_ops(0)
    first1, full1, diag1 = head_ops(1)
    @pl.when(t == 0)
    def _():
        first0(s0_ref)
        diag0(s0_ref, s1_ref, between=lambda: first1(s2_ref))
        diag1(s2_ref, s1_ref)

    @pl.when(t > 0)
    def _():
        first0(s0_ref)
        full0(s0_ref, s1_ref)
        diag0(s0_ref, s1_ref, between=lambda: first1(s2_ref))
        full1(s2_ref, s1_ref)
        diag1(s2_ref, s1_ref)


def _attn(qt, k1, k2, vt, lq1, lk1, lq2, lk2, g_subln, *, tb):
    B, S, D = k1.shape
    nb = S // tb
    hp = 2
    lam_spec = pl.BlockSpec((1, DA_HEAD_DIM), lambda b, h, i: (0, 0))
    kspec = pl.BlockSpec((None, S, hp * V7X_LANES), lambda b, h, i: (b, 0, h))
    steps = nb // 2
    stat = pltpu.VMEM((hp, 2, 1, 2 * tb), F32)
    return pl.pallas_call(
        functools.partial(_attn_kernel, tb=tb),
        out_shape=jax.ShapeDtypeStruct((B, S, D), BF16),
        grid=(B, DA_HEADS // hp, steps),
        in_specs=[lam_spec] * 4 + [
            pl.BlockSpec((V7X_LANES, 1), lambda b, h, i: (0, 0)),
            pl.BlockSpec((None, 2, hp * V7X_LANES, tb), lambda b, h, i: (b, i, h, 0)),
            kspec, kspec,
            pl.BlockSpec((None, nb, hp * VT_ROWS, tb), lambda b, h, i: (b, 0, h, 0))],
        out_specs=pl.BlockSpec((None, 2 * tb, hp * V7X_LANES), lambda b, h, i: (b, i, h)),
        scratch_shapes=[pltpu.VMEM((2, tb, 2 * tb), F32)] * 3 + [
            stat, stat, pltpu.VMEM((hp, 2, V7X_LANES, 2 * tb), F32)],
        compiler_params=_params("parallel", "parallel", "arbitrary"),
        name="diff_attn",
    )(lq1, lk1, lq2, lk2, g_subln.reshape(V7X_LANES, 1), qt, k1, k2, vt)


def _scan_rows(x, op, identity):
    n = x.shape[0]
    row = lax.broadcasted_iota(jnp.int32, x.shape, 0)
    shift = 1
    while shift < n:
        x = op(x, jnp.where(row >= shift, pltpu.roll(x, shift, 0), identity))
        shift *= 2
    return x


def _lane(x, h):
    return x[:, h:h + 1]


def _mlstm_kernel(xm_ref, op_ref, cw_ref, cb_ref, wq_ref, wk_ref, wv_ref,
                  wif_ref, bif_ref, skip_ref, gn_ref, yb_ref,
                  xbuf, c_st, n_st, m_st, *, L, nseq):
    H, Dh = ML_HEADS, ML_HEAD_DIM
    halo = V7X_SUBLANES

    @pl.when(pl.program_id(1) == 0)
    def _():
        xbuf[...] = jnp.zeros_like(xbuf)
        c_st[...] = jnp.zeros_like(c_st)
        n_st[...] = jnp.zeros_like(n_st)
        m_st[...] = jnp.zeros_like(m_st)

    def front(s):
        xm = xm_ref[s]
        xext = jnp.concatenate([xbuf[s], xm], axis=0)
        conv = cb_ref[...] + cw_ref[CONV_WIDTH - 1:CONV_WIDTH, :] * xm
        for d in range(1, CONV_WIDTH):
            conv = conv + (cw_ref[CONV_WIDTH - 1 - d:CONV_WIDTH - d, :]
                           * pltpu.roll(xext, d, 0)[halo:, :])
        xbuf[s] = xm[L - halo:L, :]
        xc = conv * _sigmoid(conv)
        qs, ks, vs = [], [], []
        gates = jnp.broadcast_to(bif_ref[...], (2 * V7X_SUBLANES, L))
        for h in range(H):
            sl = slice(h * Dh, (h + 1) * Dh)
            xc_h = xc[:, sl].astype(BF16)
            q = jnp.dot(xc_h, wq_ref[h], preferred_element_type=F32)
            k = jnp.dot(xc_h, wk_ref[h], preferred_element_type=F32)
            v = jnp.dot(xm[:, sl].astype(BF16), wv_ref[h], preferred_element_type=F32)
            for j, t in enumerate((q, k, v)):
                gates = gates + lax.dot_general(wif_ref[j, h], t.astype(BF16), nt,
                                                preferred_element_type=F32)
            qs.append(q)
            ks.append(k)
            vs.append(v)
        return xc, qs, ks, vs, gates

    def gate_math(s, gates):
        pad = jnp.zeros((V7X_LANES - V7X_SUBLANES, L), F32)

        def columns(rows):
            return jnp.concatenate([rows, pad], axis=0).T

        gi = columns(gates[:V7X_SUBLANES])
        b = _scan_rows(columns(jax.nn.log_sigmoid(gates[V7X_SUBLANES:])), jnp.add, 0.0)
        a = gi - b
        m_prev = m_st[s]
        mx = jnp.maximum(m_prev, _scan_rows(a, jnp.maximum, -jnp.inf))
        b_last = b[L - 1:L, :]
        m_new = b_last + mx[L - 1:L, :]
        decay = jnp.exp(b_last + m_prev - m_new)
        wg = jnp.exp(b_last + a - m_new)
        m_st[s] = m_new
        return (-mx, jnp.exp(m_prev - mx), wg, jnp.exp(-(b + mx))), a.T, decay

    tril = (lax.broadcasted_iota(jnp.int32, (L, L), 1)
            <= lax.broadcasted_iota(jnp.int32, (L, L), 0))
    nt = (((1,), (1,)), ((), ()))
    tn = (((0,), (0,)), ((), ()))

    def head(s, h, xc, q, k, v, cols, a, decay):
        neg_mx, w_carry, w_state, exp_neg_m = cols
        sl = slice(h * Dh, (h + 1) * Dh)
        qb, kb, vb = q.astype(BF16), k.astype(BF16), v.astype(BF16)
        w = jnp.exp(jnp.where(tril, _lane(neg_mx, h) + a[h:h + 1, :], -jnp.inf))
        sc = lax.dot_general(qb, kb, nt, preferred_element_type=F32) * w
        w_inter = _lane(w_carry, h)
        c_prev = c_st[s, h]
        n_prev = n_st[s, h]
        num = (jnp.dot(sc.astype(BF16), vb, preferred_element_type=F32)
               + w_inter * jnp.dot(qb, c_prev.astype(BF16), preferred_element_type=F32))
        den = (jnp.sum(sc, axis=-1, keepdims=True)
               + w_inter * jnp.sum(q * n_prev, axis=-1, keepdims=True))
        hcell = num / jnp.maximum(jnp.abs(den), _lane(exp_neg_m, h))

        wk = _lane(w_state, h) * k
        dec = _lane(decay, h)
        c_st[s, h] = dec * c_prev + lax.dot_general(wk.astype(BF16), vb, tn,
                                                    preferred_element_type=F32)
        n_st[s, h] = dec * n_prev + jnp.sum(wk, axis=0, keepdims=True)

        ms = jnp.mean(hcell * hcell, axis=-1, keepdims=True)
        hn = hcell * lax.rsqrt(ms + EPS) * gn_ref[:, sl]
        y = ((hn + skip_ref[:, sl] * xc[:, sl])
             * _sigmoid(op_ref[s, :, sl].astype(F32)))
        yb_ref[s, :, sl] = y.astype(yb_ref.dtype)

    fronts, maths = [], []
    for s in range(nseq):
        fronts.append(front(s))
        maths.append(gate_math(s, fronts[s][4]))
    for h in range(H):
        for s in range(nseq):
            xc, qs, ks, vs, _ = fronts[s]
            head(s, h, xc, qs[h], ks[h], vs[h], *maths[s])


def _mlstm(xm, o_pre, conv_w, conv_b, wq, wk, wv, wif, bif, ml_skip, g_mlnorm,
           *, L=256, nseq=2):
    B, S, D = xm.shape
    H, Dh = ML_HEADS, ML_HEAD_DIM
    row = pl.BlockSpec((nseq, L, D), lambda b, i: (b, i, 0))
    return pl.pallas_call(
        functools.partial(_mlstm_kernel, L=L, nseq=nseq),
        out_shape=jax.ShapeDtypeStruct((B, S, D), BF16),
        grid=(B // nseq, S // L),
        in_specs=[row, row, _resident((CONV_WIDTH, D)), _resident((1, D)),
                  _resident((H, Dh, Dh)), _resident((H, Dh, Dh)), _resident((H, Dh, Dh)),
                  _resident((3, H, 2 * V7X_SUBLANES, Dh)), _resident((2 * V7X_SUBLANES, 1)),
                  _resident((1, D)), _resident((1, D))],
        out_specs=row,
        scratch_shapes=[pltpu.VMEM((nseq, V7X_SUBLANES, D), F32),
                        pltpu.VMEM((nseq, H, Dh, Dh), F32),
                        pltpu.VMEM((nseq, H, 1, Dh), F32),
                        pltpu.VMEM((nseq, 1, V7X_LANES), F32)],
        compiler_params=_params("parallel", "arbitrary"),
        name="mlstm",
    )(xm, o_pre, conv_w, conv_b.reshape(1, D), wq, wk, wv, wif, bif,
      ml_skip.reshape(1, D), g_mlnorm.reshape(1, D))


def _merge_ffn_kernel(h_ref, gt2_ref, ya_ref, yb_ref, ga_ref, gb_ref,
                      wa_ref, wb_ref, wo_ref, sh_ref, sc_ref, gt_ref, g_ref,
                      wg_ref, wu_ref, wd_ref, gfin_ref, o_ref):
    pa = jnp.dot(ya_ref[...], wa_ref[...], preferred_element_type=F32)
    pb = jnp.dot(yb_ref[...], wb_ref[...], preferred_element_type=F32)
    merged = (_sigmoid(ga_ref[...].astype(F32)) * pa
              + _sigmoid(gb_ref[...].astype(F32)) * pb)
    y = jnp.dot(merged.astype(BF16), wo_ref[...], preferred_element_type=F32)
    x = h_ref[...] + gt2_ref[...] * y
    out = _swiglu_residual(x, sh_ref, sc_ref, gt_ref, g_ref, wg_ref, wu_ref, wd_ref)
    ms = jnp.mean(out * out, axis=-1, keepdims=True)
    o_ref[...] = out * lax.rsqrt(ms + EPS) * gfin_ref[...]


def _merge_ffn(h, mods, ya, yb, ga, gb, wa, wb, wo, g, wg, wu, wd, g_final, *, tm=512):
    B, S, D = h.shape
    row = pl.BlockSpec((None, tm, D), lambda b, i: (b, i, 0))
    return pl.pallas_call(
        _merge_ffn_kernel,
        out_shape=jax.ShapeDtypeStruct((B, S, D), F32),
        grid=(B, S // tm),
        in_specs=[row, _mod_spec(5), row, row, row, row,
                  _resident((D, D)), _resident((D, D)), _resident((D, D)),
                  _mod_spec(6), _mod_spec(7), _mod_spec(8)]
        + _ffn_weight_specs() + [_resident((1, D))],
        out_specs=row,
        compiler_params=_params("parallel", "parallel"),
        name="merge_ffn",
    )(h, mods, ya, yb, ga, gb, wa, wb, wo, mods, mods, mods,
      g.reshape(1, D), wg, wu, wd, g_final.reshape(1, D))


def _rope_tables(S):
    inv = ROPE_THETA ** (-jnp.arange(0, DA_HEAD_DIM, 2, dtype=F32) / DA_HEAD_DIM)
    ang = jnp.arange(S, dtype=F32)[:, None] * inv[None, :]
    cos, sin = jnp.cos(ang), jnp.sin(ang)
    cos_g = jnp.concatenate([cos, cos], axis=-1)
    sin_g = jnp.concatenate([-sin, sin], axis=-1)
    reps = V7X_LANES // DA_HEAD_DIM
    return (jnp.tile(cos_g, (1, reps)), jnp.tile(sin_g, (1, reps)), cos_g.T, sin_g.T)


def _gate_weights(w_if, b_if):
    H = ML_HEADS
    pad = V7X_SUBLANES - H
    wt = jnp.swapaxes(w_if, -1, -2)
    wt = wt * jnp.array([1.0, ML_HEAD_DIM ** 0.5, 1.0], F32)[:, None, None, None]
    rows = ((0, 0),) * 2 + ((0, pad), (0, 0))
    wt = jnp.concatenate([jnp.pad(wt[:, :, :H], rows), jnp.pad(wt[:, :, H:], rows)], axis=2)
    bias = jnp.concatenate([jnp.pad(b_if[:H], (0, pad)), jnp.pad(b_if[H:], (0, pad))])
    return wt.astype(BF16), bias.reshape(2 * V7X_SUBLANES, 1)


def kernel(x, c, w_ada, b_ada, g_ff1, w1_gate, w1_up, w1_down, g_mix, w_in,
           lambda_q1, lambda_k1, lambda_q2, lambda_k2, g_subln, conv_w, conv_b,
           w_mq, w_mk, w_mv, w_if, b_if, ml_skip, g_mlnorm, w_proj_a, w_proj_b,
           w_out, g_ff2, w2_gate, w2_up, w2_down, g_final):
    B, S, D = x.shape
    l = 0
    mods = _ada(c, w_ada[l], b_ada[l]).reshape(N_MOD, B, 1, D)
    tables = _rope_tables(S)
    wif, bif = _gate_weights(w_if[l], b_if[l])
    w_in_bf = w_in[l].astype(BF16)
    wqt = w_in_bf[:, :D].T
    wk = w_in_bf[:, D:2 * D]
    wvt = w_in_bf[:, 2 * D:3 * D].T
    wr = w_in_bf[:, 3 * D:]

    h = _ffn(x, mods, 0, g_ff1[l], w1_gate[l].astype(BF16), w1_up[l].astype(BF16),
             w1_down[l].astype(BF16))
    qt, k1, k2, vt, xm, o_pre, ga, gb = _proj(h, mods, g_mix[l], wqt, wk, wvt, wr, tables,
                                              tm=ATTN_BLOCK)
    ya = _attn(qt, k1, k2, vt, lambda_q1[l].reshape(1, -1), lambda_k1[l].reshape(1, -1),
               lambda_q2[l].reshape(1, -1), lambda_k2[l].reshape(1, -1),
               g_subln[l], tb=ATTN_BLOCK)
    yb = _mlstm(xm, o_pre, conv_w[l], conv_b[l], w_mq[l].astype(BF16),
                (w_mk[l] * ML_HEAD_DIM ** -0.5).astype(BF16), w_mv[l].astype(BF16), wif, bif,
                ml_skip[l], g_mlnorm[l])
    return _merge_ffn(h, mods, ya, yb, ga, gb, w_proj_a[l].astype(BF16),
                      w_proj_b[l].astype(BF16), w_out[l].astype(BF16), g_ff2[l],
                      w2_gate[l].astype(BF16), w2_up[l].astype(BF16),
                      w2_down[l].astype(BF16), g_final)
```

```python
import functools
import math

import jax
import jax.numpy as jnp
from jax import lax
from jax.experimental import pallas as pl
from jax.experimental.pallas import tpu as pltpu

D_MODEL = 1024
CHUNK = 64
EPS = 1e-6
DA_HEADS = 8
DA_HEAD_DIM = 64
ROPE_THETA = 10000.0
ML_HEADS = 4
ML_HEAD_DIM = 256
CONV_WIDTH = 4
D_FF = 2816
FFN_RES = 0.5
N_MOD = 9
LAMBDA_INIT = 0.8 - 0.6 * math.exp(-0.3 * 0)

V7X_LANES = 128
V7X_SUBLANES = 8
V7X_MXU_COLS = 256
V7X_VMEM_LIMIT_BYTES = 56 * 1024 * 1024

ATTN_BLOCK = 512
VT_ROWS = 128 + 16

BF16 = jnp.bfloat16
F32 = jnp.float32


def _resident(shape):
    nd = len(shape)
    return pl.BlockSpec(shape, lambda *_: (0,) * nd, pipeline_mode=pl.Buffered(1))


def _params(*sem):
    return pltpu.CompilerParams(dimension_semantics=sem,
                                vmem_limit_bytes=V7X_VMEM_LIMIT_BYTES)


def _mod_spec(k):
    return pl.BlockSpec((None, None, 1, D_MODEL), lambda b, i: (k, b, 0, 0))


def _rms_mod(x, g, shift, scale):
    ms = jnp.mean(x * x, axis=-1, keepdims=True)
    xn = x * lax.rsqrt(ms + EPS) * g
    return xn * (1.0 + scale) + shift


def _sigmoid(x):
    return 1.0 / (1.0 + jnp.exp(-x))


def _ada_kernel(c_ref, w_ref, b_ref, o_ref):
    c = c_ref[...]
    a = c * _sigmoid(c)
    o_ref[...] = jnp.dot(a, w_ref[...], preferred_element_type=F32) + b_ref[...]


def _ada(c, w_ada, b_ada):
    B = c.shape[0]
    return pl.pallas_call(
        _ada_kernel,
        out_shape=jax.ShapeDtypeStruct((N_MOD, B, D_MODEL), F32),
        grid=(N_MOD,),
        in_specs=[
            pl.BlockSpec((B, D_MODEL), lambda j: (0, 0)),
            pl.BlockSpec((D_MODEL, D_MODEL), lambda j: (0, j)),
            pl.BlockSpec((None, 1, D_MODEL), lambda j: (j, 0, 0)),
        ],
        out_specs=pl.BlockSpec((None, B, D_MODEL), lambda j: (j, 0, 0)),
        compiler_params=_params("arbitrary"),
        name="ada",
    )(c, w_ada, b_ada.reshape(N_MOD, 1, D_MODEL))


def _swiglu_residual(x, sh_ref, sc_ref, gt_ref, g_ref, wg_ref, wu_ref, wd_ref):
    u = _rms_mod(x, g_ref[...], sh_ref[...], sc_ref[...]).astype(BF16)
    gate = jnp.dot(u, wg_ref[...], preferred_element_type=F32)
    up = jnp.dot(u, wu_ref[...], preferred_element_type=F32)
    a = (gate * _sigmoid(gate) * up).astype(BF16)
    y = jnp.dot(a, wd_ref[...], preferred_element_type=F32)
    return x + FFN_RES * gt_ref[...] * y


def _ffn_kernel(h_ref, sh_ref, sc_ref, gt_ref, g_ref, wg_ref, wu_ref, wd_ref, o_ref):
    o_ref[...] = _swiglu_residual(h_ref[...], sh_ref, sc_ref, gt_ref, g_ref,
                                  wg_ref, wu_ref, wd_ref)


def _ffn_weight_specs():
    return [_resident((1, D_MODEL)), _resident((D_MODEL, D_FF)),
            _resident((D_MODEL, D_FF)), _resident((D_FF, D_MODEL))]


def _ffn(h, mods, k0, g, wg, wu, wd, *, tm=512):
    B, S, D = h.shape
    row = pl.BlockSpec((None, tm, D), lambda b, i: (b, i, 0))
    return pl.pallas_call(
        _ffn_kernel,
        out_shape=jax.ShapeDtypeStruct((B, S, D), F32),
        grid=(B, S // tm),
        in_specs=[row, _mod_spec(k0), _mod_spec(k0 + 1), _mod_spec(k0 + 2)]
        + _ffn_weight_specs(),
        out_specs=row,
        compiler_params=_params("parallel", "parallel"),
        name="ffn",
    )(h, mods, mods, mods, g.reshape(1, D), wg, wu, wd)


def _rope_rows(t, cos, sin_signed):
    half = DA_HEAD_DIM // 2
    partner = jnp.concatenate([t[half:], t[:half]], axis=0)
    return t * cos + partner * sin_signed


def _rope_lanes(t, cos, sin_signed, first_half):
    outs = []
    for gidx in range(t.shape[1] // V7X_LANES):
        x = t[:, gidx * V7X_LANES:(gidx + 1) * V7X_LANES]
        partner = jnp.where(first_half,
                            pltpu.roll(x, V7X_LANES - DA_HEAD_DIM // 2, 1),
                            pltpu.roll(x, DA_HEAD_DIM // 2, 1))
        outs.append(x * cos + partner * sin_signed)
    return outs


def _proj_kernel(h_ref, sh_ref, sc_ref, g_ref, wqt_ref, wk_ref, wvt_ref, wr_ref,
                 cos_ref, sin_ref, cost_ref, sint_ref,
                 qt_ref, k1_ref, k2_ref, vt_ref, xm_ref, op_ref, ga_ref, gb_ref):
    x = h_ref[...]
    u = _rms_mod(x, g_ref[...], sh_ref[...], sc_ref[...]).astype(BF16)
    nt = (((1,), (1,)), ((), ()))

    q_scale = DA_HEAD_DIM ** -0.5 * math.log2(math.e)
    qt = lax.dot_general(wqt_ref[...], u, nt, preferred_element_type=F32)
    cost = cost_ref[...]
    sint = sint_ref[...]
    for gidx in range(D_MODEL // DA_HEAD_DIM):
        rows = slice(gidx * DA_HEAD_DIM, (gidx + 1) * DA_HEAD_DIM)
        qt_ref[rows, :] = (_rope_rows(qt[rows], cost, sint) * q_scale).astype(BF16)

    lane = lax.broadcasted_iota(jnp.int32, (x.shape[0], V7X_LANES), 1)
    first_half = (lane % DA_HEAD_DIM) < (DA_HEAD_DIM // 2)
    comp0 = lane < DA_HEAD_DIM
    k = jnp.dot(u, wk_ref[...], preferred_element_type=F32)
    for gidx, t in enumerate(_rope_lanes(k, cos_ref[...], sin_ref[...], first_half)):
        sl = slice(gidx * V7X_LANES, (gidx + 1) * V7X_LANES)
        k1_ref[:, sl] = jnp.where(comp0, t, 0.0).astype(BF16)
        k2_ref[:, sl] = jnp.where(comp0, 0.0, t).astype(BF16)

    vt = lax.dot_general(wvt_ref[...], u, nt, preferred_element_type=F32).astype(BF16)
    extra = VT_ROWS - V7X_LANES
    ones_rows = (lax.broadcasted_iota(jnp.int32, (extra, x.shape[0]), 0) == 0).astype(BF16)
    for hd in range(DA_HEADS):
        vt_ref[hd * VT_ROWS:hd * VT_ROWS + V7X_LANES, :] = (
            vt[hd * V7X_LANES:(hd + 1) * V7X_LANES])
        vt_ref[hd * VT_ROWS + V7X_LANES:(hd + 1) * VT_ROWS, :] = ones_rows
    for s, ref in enumerate((xm_ref, op_ref, ga_ref, gb_ref)):
        ref[...] = jnp.dot(u, wr_ref[:, s * D_MODEL:(s + 1) * D_MODEL],
                           preferred_element_type=F32).astype(ref.dtype)


def _proj(h, mods, g, wqt, wk, wvt, wr, tables, *, tm):
    B, S, D = h.shape
    cos, sin, cost, sint = tables
    row = pl.BlockSpec((None, tm, D), lambda b, i: (b, i, 0))
    col = pl.BlockSpec((None, None, D, tm), lambda b, i: (b, i, 0, 0))
    tab = pl.BlockSpec((tm, V7X_LANES), lambda b, i: (i, 0))
    tabt = pl.BlockSpec((DA_HEAD_DIM, tm), lambda b, i: (0, i))
    bf = jax.ShapeDtypeStruct((B, S, D), BF16)
    bft = jax.ShapeDtypeStruct((B, S // tm, D, tm), BF16)
    vrows = DA_HEADS * VT_ROWS
    vspec = pl.BlockSpec((None, None, vrows, tm), lambda b, i: (b, i, 0, 0))
    bfv = jax.ShapeDtypeStruct((B, S // tm, vrows, tm), BF16)
    f32 = jax.ShapeDtypeStruct((B, S, D), F32)
    return pl.pallas_call(
        _proj_kernel,
        out_shape=(bft, bf, bf, bfv, f32, bf, bf, bf),
        grid=(B, S // tm),
        in_specs=[row, _mod_spec(3), _mod_spec(4), _resident((1, D)),
                  _resident((D, D)), _resident((D, D)), _resident((D, D)),
                  _resident((D, 4 * D)), tab, tab, tabt, tabt],
        out_specs=(col, row, row, vspec, row, row, row, row),
        compiler_params=_params("parallel", "parallel"),
        name="mixer_proj",
    )(h, mods, mods, g.reshape(1, D), wqt, wk, wvt, wr, cos, sin, cost, sint)


def _attn_kernel(lq1_ref, lk1_ref, lq2_ref, lk2_ref, gsub_ref,
                 qt_ref, k1_ref, k2_ref, vt_ref, o_ref, s0_ref, s1_ref, s2_ref,
                 m_ref, l_ref, acc_ref, *, tb):
    t = pl.program_id(2)
    wide = 2 * tb
    lam = (jnp.exp(jnp.sum(lq1_ref[...] * lk1_ref[...], axis=-1, keepdims=True))
           - jnp.exp(jnp.sum(lq2_ref[...] * lk2_ref[...], axis=-1, keepdims=True))
           + LAMBDA_INIT)

    def head_ops(hd):
        lanes = slice(hd * V7X_LANES, (hd + 1) * V7X_LANES)
        qt = jnp.concatenate([qt_ref[0, lanes, :], qt_ref[1, lanes, :]],
                             axis=1)
        k_refs = (k1_ref, k2_ref)

        def scores(c, j, s_ref, c0=0):
            start = pl.multiple_of(j * tb, tb)
            s_ref[c, :, c0:] = jnp.dot(k_refs[c][pl.ds(start, tb), lanes], qt[:, c0:],
                                       preferred_element_type=F32)

        def softmax_pv(c, j, s_ref, c0=0, key_off=None):
            vt = vt_ref[j, hd * VT_ROWS:(hd + 1) * VT_ROWS, :]
            s = s_ref[c, :, c0:]
            if key_off is not None:
                shape = (tb, wide - c0)
                kc = (lax.broadcasted_iota(jnp.int32, shape, 0) + key_off) // CHUNK
                qc = (lax.broadcasted_iota(jnp.int32, shape, 1) + c0) // CHUNK
                s = jnp.where(kc <= qc, s, -jnp.inf)
            m = m_ref[hd, c, :, c0:]
            m_new = jnp.maximum(m, jnp.max(s, axis=0, keepdims=True))
            alpha = jnp.exp2(m - m_new)
            p = jnp.exp2(s - m_new)
            pv = jnp.dot(vt, p.astype(BF16), preferred_element_type=F32)
            m_ref[hd, c, :, c0:] = m_new
            l_ref[hd, c, :, c0:] = (alpha * l_ref[hd, c, :, c0:]
                                    + pv[V7X_LANES:V7X_LANES + 1])
            acc_ref[hd, c, :, c0:] = alpha * acc_ref[hd, c, :, c0:] + pv[:V7X_LANES]

        def first_scores(slot):
            m_ref[hd] = jnp.full(m_ref.shape[1:], -jnp.inf, F32)
            l_ref[hd] = jnp.zeros(l_ref.shape[1:], F32)
            acc_ref[hd] = jnp.zeros(acc_ref.shape[1:], F32)
            for c in range(2):
                scores(c, 0, slot)

        def block_pair(u, slot_a, slot_b):
            for cur, nxt, j in ((slot_a, slot_b, 2 * u), (slot_b, slot_a, 2 * u + 1)):
                for c in range(2):
                    scores(c, j + 1, nxt)
                    softmax_pv(c, j, cur)

        def full_blocks(slot_a, slot_b):
            block_pair(0, slot_a, slot_b)
            pl.loop(1, t)(lambda u: block_pair(u, slot_a, slot_b))

        def diagonal(slot_a, slot_b, between=None):
            for c in range(2):
                scores(c, 2 * t + 1, slot_b, c0=tb)
                softmax_pv(c, 2 * t, slot_a, key_off=0)
            if between is not None:
                between()
            for c in range(2):
                softmax_pv(c, 2 * t + 1, slot_b, c0=tb, key_off=tb)
            o = (acc_ref[hd, 0] * (1.0 / l_ref[hd, 0])
                 - acc_ref[hd, 1] * (lam / l_ref[hd, 1]))
            ms = jnp.mean(o * o, axis=0, keepdims=True)
            o = o * lax.rsqrt(ms + EPS) * (gsub_ref[...] * (1.0 - LAMBDA_INIT))
            o_ref[:, lanes] = o.T.astype(o_ref.dtype)

        return first_scores, full_blocks, diagonal

    first0, full0, diag0 = head_ops(0)
    first1, full1, diag1 = head_ops(1)
    @pl.when(t == 0)
    def _():
        first0(s0_ref)
        diag0(s0_ref, s1_ref, between=lambda: first1(s2_ref))
        diag1(s2_ref, s1_ref)

    @pl.when(t > 0)
    def _():
        first0(s0_ref)
        full0(s0_ref, s1_ref)
        diag0(s0_ref, s1_ref, between=lambda: first1(s2_ref))
        full1(s2_ref, s1_ref)
        diag1(s2_ref, s1_ref)


def _attn(qt, k1, k2, vt, lq1, lk1, lq2, lk2, g_subln, *, tb):
    B, S, D = k1.shape
    nb = S // tb
    hp = 2
    lam_spec = pl.BlockSpec((1, DA_HEAD_DIM), lambda b, h, i: (0, 0))
    kspec = pl.BlockSpec((None, S, hp * V7X_LANES), lambda b, h, i: (b, 0, h))
    steps = nb // 2
    stat = pltpu.VMEM((hp, 2, 1, 2 * tb), F32)
    return pl.pallas_call(
        functools.partial(_attn_kernel, tb=tb),
        out_shape=jax.ShapeDtypeStruct((B, S, D), BF16),
        grid=(B, DA_HEADS // hp, steps),
        in_specs=[lam_spec] * 4 + [
            pl.BlockSpec((V7X_LANES, 1), lambda b, h, i: (0, 0)),
            pl.BlockSpec((None, 2, hp * V7X_LANES, tb), lambda b, h, i: (b, i, h, 0)),
            kspec, kspec,
            pl.BlockSpec((None, nb, hp * VT_ROWS, tb), lambda b, h, i: (b, 0, h, 0))],
        out_specs=pl.BlockSpec((None, 2 * tb, hp * V7X_LANES), lambda b, h, i: (b, i, h)),
        scratch_shapes=[pltpu.VMEM((2, tb, 2 * tb), F32)] * 3 + [
            stat, stat, pltpu.VMEM((hp, 2, V7X_LANES, 2 * tb), F32)],
        compiler_params=_params("parallel", "parallel", "arbitrary"),
        name="diff_attn",
    )(lq1, lk1, lq2, lk2, g_subln.reshape(V7X_LANES, 1), qt, k1, k2, vt)


def _scan_rows(x, op, identity):
    n = x.shape[0]
    row = lax.broadcasted_iota(jnp.int32, x.shape, 0)
    shift = 1
    while shift < n:
        x = op(x, jnp.where(row >= shift, pltpu.roll(x, shift, 0), identity))
        shift *= 2
    return x


def _lane(x, h):
    return x[:, h:h + 1]


def _mlstm_kernel(xm_ref, op_ref, cw_ref, cb_ref, wq_ref, wk_ref, wv_ref,
                  wif_ref, bif_ref, skip_ref, gn_ref, yb_ref,
                  xbuf, c_st, n_st, m_st, *, L, nseq):
    H, Dh = ML_HEADS, ML_HEAD_DIM
    halo = V7X_SUBLANES

    @pl.when(pl.program_id(1) == 0)
    def _():
        xbuf[...] = jnp.zeros_like(xbuf)
        c_st[...] = jnp.zeros_like(c_st)
        n_st[...] = jnp.zeros_like(n_st)
        m_st[...] = jnp.zeros_like(m_st)

    def front(s):
        xm = xm_ref[s]
        xext = jnp.concatenate([xbuf[s], xm], axis=0)
        conv = cb_ref[...] + cw_ref[CONV_WIDTH - 1:CONV_WIDTH, :] * xm
        for d in range(1, CONV_WIDTH):
            conv = conv + (cw_ref[CONV_WIDTH - 1 - d:CONV_WIDTH - d, :]
                           * pltpu.roll(xext, d, 0)[halo:, :])
        xbuf[s] = xm[L - halo:L, :]
        xc = conv * _sigmoid(conv)
        qs, ks, vs = [], [], []
        gates = jnp.broadcast_to(bif_ref[...], (2 * V7X_SUBLANES, L))
        for h in range(H):
            sl = slice(h * Dh, (h + 1) * Dh)
            xc_h = xc[:, sl].astype(BF16)
            q = jnp.dot(xc_h, wq_ref[h], preferred_element_type=F32)
            k = jnp.dot(xc_h, wk_ref[h], preferred_element_type=F32)
            v = jnp.dot(xm[:, sl].astype(BF16), wv_ref[h], preferred_element_type=F32)
            for j, t in enumerate((q, k, v)):
                gates = gates + lax.dot_general(wif_ref[j, h], t.astype(BF16), nt,
                                                preferred_element_type=F32)
            qs.append(q)
            ks.append(k)
            vs.append(v)
        return xc, qs, ks, vs, gates

    def gate_math(s, gates):
        pad = jnp.zeros((V7X_LANES - V7X_SUBLANES, L), F32)

        def columns(rows):
            return jnp.concatenate([rows, pad], axis=0).T

        gi = columns(gates[:V7X_SUBLANES])
        b = _scan_rows(columns(jax.nn.log_sigmoid(gates[V7X_SUBLANES:])), jnp.add, 0.0)
        a = gi - b
        m_prev = m_st[s]
        mx = jnp.maximum(m_prev, _scan_rows(a, jnp.maximum, -jnp.inf))
        b_last = b[L - 1:L, :]
        m_new = b_last + mx[L - 1:L, :]
        decay = jnp.exp(b_last + m_prev - m_new)
        wg = jnp.exp(b_last + a - m_new)
        m_st[s] = m_new
        return (-mx, jnp.exp(m_prev - mx), wg, jnp.exp(-(b + mx))), a.T, decay

    tril = (lax.broadcasted_iota(jnp.int32, (L, L), 1)
            <= lax.broadcasted_iota(jnp.int32, (L, L), 0))
    nt = (((1,), (1,)), ((), ()))
    tn = (((0,), (0,)), ((), ()))

    def head(s, h, xc, q, k, v, cols, a, decay):
        neg_mx, w_carry, w_state, exp_neg_m = cols
        sl = slice(h * Dh, (h + 1) * Dh)
        qb, kb, vb = q.astype(BF16), k.astype(BF16), v.astype(BF16)
        w = jnp.exp(jnp.where(tril, _lane(neg_mx, h) + a[h:h + 1, :], -jnp.inf))
        sc = lax.dot_general(qb, kb, nt, preferred_element_type=F32) * w
        w_inter = _lane(w_carry, h)
        c_prev = c_st[s, h]
        n_prev = n_st[s, h]
        num = (jnp.dot(sc.astype(BF16), vb, preferred_element_type=F32)
               + w_inter * jnp.dot(qb, c_prev.astype(BF16), preferred_element_type=F32))
        den = (jnp.sum(sc, axis=-1, keepdims=True)
               + w_inter * jnp.sum(q * n_prev, axis=-1, keepdims=True))
        hcell = num / jnp.maximum(jnp.abs(den), _lane(exp_neg_m, h))

        wk = _lane(w_state, h) * k
        dec = _lane(decay, h)
        c_st[s, h] = dec * c_prev + lax.dot_general(wk.astype(BF16), vb, tn,
                                                    preferred_element_type=F32)
        n_st[s, h] = dec * n_prev + jnp.sum(wk, axis=0, keepdims=True)

        ms = jnp.mean(hcell * hcell, axis=-1, keepdims=True)
        hn = hcell * lax.rsqrt(ms + EPS) * gn_ref[:, sl]
        y = ((hn + skip_ref[:, sl] * xc[:, sl])
             * _sigmoid(op_ref[s, :, sl].astype(F32)))
        yb_ref[s, :, sl] = y.astype(yb_ref.dtype)

    fronts, maths = [], []
    for s in range(nseq):
        fronts.append(front(s))
        maths.append(gate_math(s, fronts[s][4]))
    for h in range(H):
        for s in range(nseq):
            xc, qs, ks, vs, _ = fronts[s]
            head(s, h, xc, qs[h], ks[h], vs[h], *maths[s])


def _mlstm(xm, o_pre, conv_w, conv_b, wq, wk, wv, wif, bif, ml_skip, g_mlnorm,
           *, L=256, nseq=2):
    B, S, D = xm.shape
    H, Dh = ML_HEADS, ML_HEAD_DIM
    row = pl.BlockSpec((nseq, L, D), lambda b, i: (b, i, 0))
    return pl.pallas_call(
        functools.partial(_mlstm_kernel, L=L, nseq=nseq),
        out_shape=jax.ShapeDtypeStruct((B, S, D), BF16),
        grid=(B // nseq, S // L),
        in_specs=[row, row, _resident((CONV_WIDTH, D)), _resident((1, D)),
                  _resident((H, Dh, Dh)), _resident((H, Dh, Dh)), _resident((H, Dh, Dh)),
                  _resident((3, H, 2 * V7X_SUBLANES, Dh)), _resident((2 * V7X_SUBLANES, 1)),
                  _resident((1, D)), _resident((1, D))],
        out_specs=row,
        scratch_shapes=[pltpu.VMEM((nseq, V7X_SUBLANES, D), F32),
                        pltpu.VMEM((nseq, H, Dh, Dh), F32),
                        pltpu.VMEM((nseq, H, 1, Dh), F32),
                        pltpu.VMEM((nseq, 1, V7X_LANES), F32)],
        compiler_params=_params("parallel", "arbitrary"),
        name="mlstm",
    )(xm, o_pre, conv_w, conv_b.reshape(1, D), wq, wk, wv, wif, bif,
      ml_skip.reshape(1, D), g_mlnorm.reshape(1, D))


def _merge_ffn_kernel(h_ref, gt2_ref, ya_ref, yb_ref, ga_ref, gb_ref,
                      wa_ref, wb_ref, wo_ref, sh_ref, sc_ref, gt_ref, g_ref,
                      wg_ref, wu_ref, wd_ref, gfin_ref, o_ref):
    pa = jnp.dot(ya_ref[...], wa_ref[...], preferred_element_type=F32)
    pb = jnp.dot(yb_ref[...], wb_ref[...], preferred_element_type=F32)
    merged = (_sigmoid(ga_ref[...].astype(F32)) * pa
              + _sigmoid(gb_ref[...].astype(F32)) * pb)
    y = jnp.dot(merged.astype(BF16), wo_ref[...], preferred_element_type=F32)
    x = h_ref[...] + gt2_ref[...] * y
    out = _swiglu_residual(x, sh_ref, sc_ref, gt_ref, g_ref, wg_ref, wu_ref, wd_ref)
    ms = jnp.mean(out * out, axis=-1, keepdims=True)
    o_ref[...] = out * lax.rsqrt(ms + EPS) * gfin_ref[...]


def _merge_ffn(h, mods, ya, yb, ga, gb, wa, wb, wo, g, wg, wu, wd, g_final, *, tm=512):
    B, S, D = h.shape
    row = pl.BlockSpec((None, tm, D), lambda b, i: (b, i, 0))
    return pl.pallas_call(
        _merge_ffn_kernel,
        out_shape=jax.ShapeDtypeStruct((B, S, D), F32),
        grid=(B, S // tm),
        in_specs=[row, _mod_spec(5), row, row, row, row,
                  _resident((D, D)), _resident((D, D)), _resident((D, D)),
                  _mod_spec(6), _mod_spec(7), _mod_spec(8)]
        + _ffn_weight_specs() + [_resident((1, D))],
        out_specs=row,
        compiler_params=_params("parallel", "parallel"),
        name="merge_ffn",
    )(h, mods, ya, yb, ga, gb, wa, wb, wo, mods, mods, mods,
      g.reshape(1, D), wg, wu, wd, g_final.reshape(1, D))


def _rope_tables(S):
    inv = ROPE_THETA ** (-jnp.arange(0, DA_HEAD_DIM, 2, dtype=F32) / DA_HEAD_DIM)
    ang = jnp.arange(S, dtype=F32)[:, None] * inv[None, :]
    cos, sin = jnp.cos(ang), jnp.sin(ang)
    cos_g = jnp.concatenate([cos, cos], axis=-1)
    sin_g = jnp.concatenate([-sin, sin], axis=-1)
    reps = V7X_LANES // DA_HEAD_DIM
    return (jnp.tile(cos_g, (1, reps)), jnp.tile(sin_g, (1, reps)), cos_g.T, sin_g.T)


def _gate_weights(w_if, b_if):
    H = ML_HEADS
    pad = V7X_SUBLANES - H
    wt = jnp.swapaxes(w_if, -1, -2)
    wt = wt * jnp.array([1.0, ML_HEAD_DIM ** 0.5, 1.0], F32)[:, None, None, None]
    rows = ((0, 0),) * 2 + ((0, pad), (0, 0))
    wt = jnp.concatenate([jnp.pad(wt[:, :, :H], rows), jnp.pad(wt[:, :, H:], rows)], axis=2)
    bias = jnp.concatenate([jnp.pad(b_if[:H], (0, pad)), jnp.pad(b_if[H:], (0, pad))])
    return wt.astype(BF16), bias.reshape(2 * V7X_SUBLANES, 1)


def kernel(x, c, w_ada, b_ada, g_ff1, w1_gate, w1_up, w1_down, g_mix, w_in,
           lambda_q1, lambda_k1, lambda_q2, lambda_k2, g_subln, conv_w, conv_b,
           w_mq, w_mk, w_mv, w_if, b_if, ml_skip, g_mlnorm, w_proj_a, w_proj_b,
           w_out, g_ff2, w2_gate, w2_up, w2_down, g_final):
    B, S, D = x.shape
    l = 0
    mods = _ada(c, w_ada[l], b_ada[l]).reshape(N_MOD, B, 1, D)
    tables = _rope_tables(S)
    wif, bif = _gate_weights(w_if[l], b_if[l])
    w_in_bf = w_in[l].astype(BF16)
    wqt = w_in_bf[:, :D].T
    wk = w_in_bf[:, D:2 * D]
    wvt = w_in_bf[:, 2 * D:3 * D].T
    wr = w_in_bf[:, 3 * D:]

    h = _ffn(x, mods, 0, g_ff1[l], w1_gate[l].astype(BF16), w1_up[l].astype(BF16),
             w1_down[l].astype(BF16))
    qt, k1, k2, vt, xm, o_pre, ga, gb = _proj(h, mods, g_mix[l], wqt, wk, wvt, wr, tables,
                                              tm=ATTN_BLOCK)
    ya = _attn(qt, k1, k2, vt, lambda_q1[l].reshape(1, -1), lambda_k1[l].reshape(1, -1),
               lambda_q2[l].reshape(1, -1), lambda_k2[l].reshape(1, -1),
               g_subln[l], tb=ATTN_BLOCK)
    yb = _mlstm(xm, o_pre, conv_w[l], conv_b[l], w_mq[l].astype(BF16),
                (w_mk[l] * ML_HEAD_DIM ** -0.5).astype(BF16), w_mv[l].astype(BF16), wif, bif,
                ml_skip[l], g_mlnorm[l])
    return _merge_ffn(h, mods, ya, yb, ga, gb, w_proj_a[l].astype(BF16),
                      w_proj_b[l].astype(BF16), w_out[l].astype(BF16), g_ff2[l],
                      w2_gate[l].astype(BF16), w2_up[l].astype(BF16),
                      w2_down[l].astype(BF16), g_final)
```

```python
import functools
import math

import jax
import jax.numpy as jnp
from jax import lax
from jax.experimental import pallas as pl
from jax.experimental.pallas import tpu as pltpu

D_MODEL = 1024
CHUNK = 64
EPS = 1e-6
DA_HEADS = 8
DA_HEAD_DIM = 64
ROPE_THETA = 10000.0
ML_HEADS = 4
ML_HEAD_DIM = 256
CONV_WIDTH = 4
D_FF = 2816
FFN_RES = 0.5
N_MOD = 9
LAMBDA_INIT = 0.8 - 0.6 * math.exp(-0.3 * 0)

V7X_LANES = 128
V7X_SUBLANES = 8
V7X_MXU_COLS = 256
V7X_VMEM_LIMIT_BYTES = 56 * 1024 * 1024

ATTN_BLOCK = 512
VT_ROWS = 128 + 16

BF16 = jnp.bfloat16
F32 = jnp.float32


def _resident(shape):
    nd = len(shape)
    return pl.BlockSpec(shape, lambda *_: (0,) * nd, pipeline_mode=pl.Buffered(1))


def _params(*sem):
    return pltpu.CompilerParams(dimension_semantics=sem,
                                vmem_limit_bytes=V7X_VMEM_LIMIT_BYTES)


def _mod_spec(k):
    return pl.BlockSpec((None, None, 1, D_MODEL), lambda b, i: (k, b, 0, 0))


def _rms_mod(x, g, shift, scale):
    ms = jnp.mean(x * x, axis=-1, keepdims=True)
    xn = x * lax.rsqrt(ms + EPS) * g
    return xn * (1.0 + scale) + shift


def _sigmoid(x):
    return 1.0 / (1.0 + jnp.exp(-x))


def _ada_kernel(c_ref, w_ref, b_ref, o_ref):
    c = c_ref[...]
    a = c * _sigmoid(c)
    o_ref[...] = jnp.dot(a, w_ref[...], preferred_element_type=F32) + b_ref[...]


def _ada(c, w_ada, b_ada):
    B = c.shape[0]
    return pl.pallas_call(
        _ada_kernel,
        out_shape=jax.ShapeDtypeStruct((N_MOD, B, D_MODEL), F32),
        grid=(N_MOD,),
        in_specs=[
            pl.BlockSpec((B, D_MODEL), lambda j: (0, 0)),
            pl.BlockSpec((D_MODEL, D_MODEL), lambda j: (0, j)),
            pl.BlockSpec((None, 1, D_MODEL), lambda j: (j, 0, 0)),
        ],
        out_specs=pl.BlockSpec((None, B, D_MODEL), lambda j: (j, 0, 0)),
        compiler_params=_params("arbitrary"),
        name="ada",
    )(c, w_ada, b_ada.reshape(N_MOD, 1, D_MODEL))


def _swiglu_residual(x, sh_ref, sc_ref, gt_ref, g_ref, wg_ref, wu_ref, wd_ref):
    u = _rms_mod(x, g_ref[...], sh_ref[...], sc_ref[...]).astype(BF16)
    gate = jnp.dot(u, wg_ref[...], preferred_element_type=F32)
    up = jnp.dot(u, wu_ref[...], preferred_element_type=F32)
    a = (gate * _sigmoid(gate) * up).astype(BF16)
    y = jnp.dot(a, wd_ref[...], preferred_element_type=F32)
    return x + FFN_RES * gt_ref[...] * y


def _ffn_kernel(h_ref, sh_ref, sc_ref, gt_ref, g_ref, wg_ref, wu_ref, wd_ref, o_ref):
    o_ref[...] = _swiglu_residual(h_ref[...], sh_ref, sc_ref, gt_ref, g_ref,
                                  wg_ref, wu_ref, wd_ref)


def _ffn_weight_specs():
    return [_resident((1, D_MODEL)), _resident((D_MODEL, D_FF)),
            _resident((D_MODEL, D_FF)), _resident((D_FF, D_MODEL))]


def _ffn(h, mods, k0, g, wg, wu, wd, *, tm=512):
    B, S, D = h.shape
    row = pl.BlockSpec((None, tm, D), lambda b, i: (b, i, 0))
    return pl.pallas_call(
        _ffn_kernel,
        out_shape=jax.ShapeDtypeStruct((B, S, D), F32),
        grid=(B, S // tm),
        in_specs=[row, _mod_spec(k0), _mod_spec(k0 + 1), _mod_spec(k0 + 2)]
        + _ffn_weight_specs(),
        out_specs=row,
        compiler_params=_params("parallel", "parallel"),
        name="ffn",
    )(h, mods, mods, mods, g.reshape(1, D), wg, wu, wd)


def _rope_rows(t, cos, sin_signed):
    half = DA_HEAD_DIM // 2
    partner = jnp.concatenate([t[half:], t[:half]], axis=0)
    return t * cos + partner * sin_signed


def _rope_lanes(t, cos, sin_signed, first_half):
    outs = []
    for gidx in range(t.shape[1] // V7X_LANES):
        x = t[:, gidx * V7X_LANES:(gidx + 1) * V7X_LANES]
        partner = jnp.where(first_half,
                            pltpu.roll(x, V7X_LANES - DA_HEAD_DIM // 2, 1),
                            pltpu.roll(x, DA_HEAD_DIM // 2, 1))
        outs.append(x * cos + partner * sin_signed)
    return outs


def _proj_kernel(h_ref, sh_ref, sc_ref, g_ref, wqt_ref, wk_ref, wvt_ref, wr_ref,
                 cos_ref, sin_ref, cost_ref, sint_ref,
                 qt_ref, k1_ref, k2_ref, vt_ref, xm_ref, op_ref, ga_ref, gb_ref):
    x = h_ref[...]
    u = _rms_mod(x, g_ref[...], sh_ref[...], sc_ref[...]).astype(BF16)
    nt = (((1,), (1,)), ((), ()))

    q_scale = DA_HEAD_DIM ** -0.5 * math.log2(math.e)
    qt = lax.dot_general(wqt_ref[...], u, nt, preferred_element_type=F32)
    cost = cost_ref[...]
    sint = sint_ref[...]
    for gidx in range(D_MODEL // DA_HEAD_DIM):
        rows = slice(gidx * DA_HEAD_DIM, (gidx + 1) * DA_HEAD_DIM)
        qt_ref[rows, :] = (_rope_rows(qt[rows], cost, sint) * q_scale).astype(BF16)

    lane = lax.broadcasted_iota(jnp.int32, (x.shape[0], V7X_LANES), 1)
    first_half = (lane % DA_HEAD_DIM) < (DA_HEAD_DIM // 2)
    comp0 = lane < DA_HEAD_DIM
    k = jnp.dot(u, wk_ref[...], preferred_element_type=F32)
    for gidx, t in enumerate(_rope_lanes(k, cos_ref[...], sin_ref[...], first_half)):
        sl = slice(gidx * V7X_LANES, (gidx + 1) * V7X_LANES)
        k1_ref[:, sl] = jnp.where(comp0, t, 0.0).astype(BF16)
        k2_ref[:, sl] = jnp.where(comp0, 0.0, t).astype(BF16)

    vt = lax.dot_general(wvt_ref[...], u, nt, preferred_element_type=F32).astype(BF16)
    extra = VT_ROWS - V7X_LANES
    ones_rows = (lax.broadcasted_iota(jnp.int32, (extra, x.shape[0]), 0) == 0).astype(BF16)
    for hd in range(DA_HEADS):
        vt_ref[hd * VT_ROWS:hd * VT_ROWS + V7X_LANES, :] = (
            vt[hd * V7X_LANES:(hd + 1) * V7X_LANES])
        vt_ref[hd * VT_ROWS + V7X_LANES:(hd + 1) * VT_ROWS, :] = ones_rows
    for s, ref in enumerate((xm_ref, op_ref, ga_ref, gb_ref)):
        ref[...] = jnp.dot(u, wr_ref[:, s * D_MODEL:(s + 1) * D_MODEL],
                           preferred_element_type=F32).astype(ref.dtype)


def _proj(h, mods, g, wqt, wk, wvt, wr, tables, *, tm):
    B, S, D = h.shape
    cos, sin, cost, sint = tables
    row = pl.BlockSpec((None, tm, D), lambda b, i: (b, i, 0))
    col = pl.BlockSpec((None, None, D, tm), lambda b, i: (b, i, 0, 0))
    tab = pl.BlockSpec((tm, V7X_LANES), lambda b, i: (i, 0))
    tabt = pl.BlockSpec((DA_HEAD_DIM, tm), lambda b, i: (0, i))
    bf = jax.ShapeDtypeStruct((B, S, D), BF16)
    bft = jax.ShapeDtypeStruct((B, S // tm, D, tm), BF16)
    vrows = DA_HEADS * VT_ROWS
    vspec = pl.BlockSpec((None, None, vrows, tm), lambda b, i: (b, i, 0, 0))
    bfv = jax.ShapeDtypeStruct((B, S // tm, vrows, tm), BF16)
    f32 = jax.ShapeDtypeStruct((B, S, D), F32)
    return pl.pallas_call(
        _proj_kernel,
        out_shape=(bft, bf, bf, bfv, f32, bf, bf, bf),
        grid=(B, S // tm),
        in_specs=[row, _mod_spec(3), _mod_spec(4), _resident((1, D)),
                  _resident((D, D)), _resident((D, D)), _resident((D, D)),
                  _resident((D, 4 * D)), tab, tab, tabt, tabt],
        out_specs=(col, row, row, vspec, row, row, row, row),
        compiler_params=_params("parallel", "parallel"),
        name="mixer_proj",
    )(h, mods, mods, g.reshape(1, D), wqt, wk, wvt, wr, cos, sin, cost, sint)


def _attn_kernel(lq1_ref, lk1_ref, lq2_ref, lk2_ref, gsub_ref,
                 qt_ref, k1_ref, k2_ref, vt_ref, o_ref, s0_ref, s1_ref, s2_ref,
                 m_ref, l_ref, acc_ref, *, tb):
    t = pl.program_id(2)
    wide = 2 * tb
    lam = (jnp.exp(jnp.sum(lq1_ref[...] * lk1_ref[...], axis=-1, keepdims=True))
           - jnp.exp(jnp.sum(lq2_ref[...] * lk2_ref[...], axis=-1, keepdims=True))
           + LAMBDA_INIT)

    def head_ops(hd):
        lanes = slice(hd * V7X_LANES, (hd + 1) * V7X_LANES)
        qt = jnp.concatenate([qt_ref[0, lanes, :], qt_ref[1, lanes, :]],
                             axis=1)
        k_refs = (k1_ref, k2_ref)

        def scores(c, j, s_ref, c0=0):
            start = pl.multiple_of(j * tb, tb)
            s_ref[c, :, c0:] = jnp.dot(k_refs[c][pl.ds(start, tb), lanes], qt[:, c0:],
                                       preferred_element_type=F32)

        def softmax_pv(c, j, s_ref, c0=0, key_off=None):
            vt = vt_ref[j, hd * VT_ROWS:(hd + 1) * VT_ROWS, :]
            s = s_ref[c, :, c0:]
            if key_off is not None:
                shape = (tb, wide - c0)
                kc = (lax.broadcasted_iota(jnp.int32, shape, 0) + key_off) // CHUNK
                qc = (lax.broadcasted_iota(jnp.int32, shape, 1) + c0) // CHUNK
                s = jnp.where(kc <= qc, s, -jnp.inf)
            m = m_ref[hd, c, :, c0:]
            m_new = jnp.maximum(m, jnp.max(s, axis=0, keepdims=True))
            alpha = jnp.exp2(m - m_new)
            p = jnp.exp2(s - m_new)
            pv = jnp.dot(vt, p.astype(BF16), preferred_element_type=F32)
            m_ref[hd, c, :, c0:] = m_new
            l_ref[hd, c, :, c0:] = (alpha * l_ref[hd, c, :, c0:]
                                    + pv[V7X_LANES:V7X_LANES + 1])
            acc_ref[hd, c, :, c0:] = alpha * acc_ref[hd, c, :, c0:] + pv[:V7X_LANES]

        def first_scores(slot):
            m_ref[hd] = jnp.full(m_ref.shape[1:], -jnp.inf, F32)
            l_ref[hd] = jnp.zeros(l_ref.shape[1:], F32)
            acc_ref[hd] = jnp.zeros(acc_ref.shape[1:], F32)
            for c in range(2):
                scores(c, 0, slot)

        def block_pair(u, slot_a, slot_b):
            for cur, nxt, j in ((slot_a, slot_b, 2 * u), (slot_b, slot_a, 2 * u + 1)):
                for c in range(2):
                    scores(c, j + 1, nxt)
                    softmax_pv(c, j, cur)

        def full_blocks(slot_a, slot_b, single):
            block_pair(0, slot_a, slot_b)
            if not single:
                pl.loop(1, t - 1)(lambda u: block_pair(u, slot_a, slot_b))
                block_pair(t - 1, slot_a, slot_b)

        def diagonal(slot_a, slot_b, between=None):
            for c in range(2):
                scores(c, 2 * t + 1, slot_b, c0=tb)
                softmax_pv(c, 2 * t, slot_a, key_off=0)
            if between is not None:
                between()
            for c in range(2):
                softmax_pv(c, 2 * t + 1, slot_b, c0=tb, key_off=tb)
            o = (acc_ref[hd, 0] * (1.0 / l_ref[hd, 0])
                 - acc_ref[hd, 1] * (lam / l_ref[hd, 1]))
            ms = jnp.mean(o * o, axis=0, keepdims=True)
            o = o * lax.rsqrt(ms + EPS) * (gsub_ref[...] * (1.0 - LAMBDA_INIT))
            o_ref[:, lanes] = o.T.astype(o_ref.dtype)

        return first_scores, full_blocks, diagonal

    first0, full0, diag0 = head_ops(0)
    first1, full1, diag1 = head_ops(1)
    @pl.when(t == 0)
    def _():
        first0(s0_ref)
        diag0(s0_ref, s1_ref, between=lambda: first1(s2_ref))
        diag1(s2_ref, s1_ref)

    def with_full_blocks(single):
        first0(s0_ref)
        full0(s0_ref, s1_ref, single)
        diag0(s0_ref, s1_ref, between=lambda: first1(s2_ref))
        full1(s2_ref, s1_ref, single)
        diag1(s2_ref, s1_ref)

    pl.when(t == 1)(lambda: with_full_blocks(True))
    pl.when(t > 1)(lambda: with_full_blocks(False))


def _attn(qt, k1, k2, vt, lq1, lk1, lq2, lk2, g_subln, *, tb):
    B, S, D = k1.shape
    nb = S // tb
    hp = 2
    lam_spec = pl.BlockSpec((1, DA_HEAD_DIM), lambda b, h, i: (0, 0))
    kspec = pl.BlockSpec((None, S, hp * V7X_LANES), lambda b, h, i: (b, 0, h))
    steps = nb // 2
    stat = pltpu.VMEM((hp, 2, 1, 2 * tb), F32)
    return pl.pallas_call(
        functools.partial(_attn_kernel, tb=tb),
        out_shape=jax.ShapeDtypeStruct((B, S, D), BF16),
        grid=(B, DA_HEADS // hp, steps),
        in_specs=[lam_spec] * 4 + [
            pl.BlockSpec((V7X_LANES, 1), lambda b, h, i: (0, 0)),
            pl.BlockSpec((None, 2, hp * V7X_LANES, tb), lambda b, h, i: (b, i, h, 0)),
            kspec, kspec,
            pl.BlockSpec((None, nb, hp * VT_ROWS, tb), lambda b, h, i: (b, 0, h, 0))],
        out_specs=pl.BlockSpec((None, 2 * tb, hp * V7X_LANES), lambda b, h, i: (b, i, h)),
        scratch_shapes=[pltpu.VMEM((2, tb, 2 * tb), F32)] * 3 + [
            stat, stat, pltpu.VMEM((hp, 2, V7X_LANES, 2 * tb), F32)],
        compiler_params=_params("parallel", "parallel", "arbitrary"),
        name="diff_attn",
    )(lq1, lk1, lq2, lk2, g_subln.reshape(V7X_LANES, 1), qt, k1, k2, vt)


def _scan_rows(x, op, identity):
    n = x.shape[0]
    row = lax.broadcasted_iota(jnp.int32, x.shape, 0)
    shift = 1
    while shift < n:
        x = op(x, jnp.where(row >= shift, pltpu.roll(x, shift, 0), identity))
        shift *= 2
    return x


def _lane(x, h):
    return x[:, h:h + 1]


def _mlstm_kernel(xm_ref, op_ref, cw_ref, cb_ref, wq_ref, wk_ref, wv_ref,
                  wif_ref, bif_ref, skip_ref, gn_ref, yb_ref,
                  xbuf, c_st, n_st, m_st, *, L, nseq):
    H, Dh = ML_HEADS, ML_HEAD_DIM
    halo = V7X_SUBLANES

    @pl.when(pl.program_id(1) == 0)
    def _():
        xbuf[...] = jnp.zeros_like(xbuf)
        c_st[...] = jnp.zeros_like(c_st)
        n_st[...] = jnp.zeros_like(n_st)
        m_st[...] = jnp.zeros_like(m_st)

    def front(s):
        xm = xm_ref[s]
        xext = jnp.concatenate([xbuf[s], xm], axis=0)
        conv = cb_ref[...] + cw_ref[CONV_WIDTH - 1:CONV_WIDTH, :] * xm
        for d in range(1, CONV_WIDTH):
            conv = conv + (cw_ref[CONV_WIDTH - 1 - d:CONV_WIDTH - d, :]
                           * pltpu.roll(xext, d, 0)[halo:, :])
        xbuf[s] = xm[L - halo:L, :]
        xc = conv * _sigmoid(conv)
        qs, ks, vs = [], [], []
        gates = jnp.broadcast_to(bif_ref[...], (2 * V7X_SUBLANES, L))
        for h in range(H):
            sl = slice(h * Dh, (h + 1) * Dh)
            xc_h = xc[:, sl].astype(BF16)
            q = jnp.dot(xc_h, wq_ref[h], preferred_element_type=F32)
            k = jnp.dot(xc_h, wk_ref[h], preferred_element_type=F32)
            v = jnp.dot(xm[:, sl].astype(BF16), wv_ref[h], preferred_element_type=F32)
            for j, t in enumerate((q, k, v)):
                gates = gates + lax.dot_general(wif_ref[j, h], t.astype(BF16), nt,
                                                preferred_element_type=F32)
            qs.append(q)
            ks.append(k)
            vs.append(v)
        return xc, qs, ks, vs, gates

    def gate_math(s, gates):
        pad = jnp.zeros((V7X_LANES - V7X_SUBLANES, L), F32)

        def columns(rows):
            return jnp.concatenate([rows, pad], axis=0).T

        gi = columns(gates[:V7X_SUBLANES])
        b = _scan_rows(columns(jax.nn.log_sigmoid(gates[V7X_SUBLANES:])), jnp.add, 0.0)
        a = gi - b
        m_prev = m_st[s]
        mx = jnp.maximum(m_prev, _scan_rows(a, jnp.maximum, -jnp.inf))
        b_last = b[L - 1:L, :]
        m_new = b_last + mx[L - 1:L, :]
        decay = jnp.exp(b_last + m_prev - m_new)
        wg = jnp.exp(b_last + a - m_new)
        m_st[s] = m_new
        return (-mx, jnp.exp(m_prev - mx), wg, jnp.exp(-(b + mx))), a.T, decay

    tril = (lax.broadcasted_iota(jnp.int32, (L, L), 1)
            <= lax.broadcasted_iota(jnp.int32, (L, L), 0))
    nt = (((1,), (1,)), ((), ()))
    tn = (((0,), (0,)), ((), ()))

    def head(s, h, xc, q, k, v, cols, a, decay):
        neg_mx, w_carry, w_state, exp_neg_m = cols
        sl = slice(h * Dh, (h + 1) * Dh)
        qb, kb, vb = q.astype(BF16), k.astype(BF16), v.astype(BF16)
        w = jnp.exp(jnp.where(tril, _lane(neg_mx, h) + a[h:h + 1, :], -jnp.inf))
        sc = lax.dot_general(qb, kb, nt, preferred_element_type=F32) * w
        w_inter = _lane(w_carry, h)
        c_prev = c_st[s, h]
        n_prev = n_st[s, h]
        num = (jnp.dot(sc.astype(BF16), vb, preferred_element_type=F32)
               + w_inter * jnp.dot(qb, c_prev.astype(BF16), preferred_element_type=F32))
        den = (jnp.sum(sc, axis=-1, keepdims=True)
               + w_inter * jnp.sum(q * n_prev, axis=-1, keepdims=True))
        hcell = num / jnp.maximum(jnp.abs(den), _lane(exp_neg_m, h))

        wk = _lane(w_state, h) * k
        dec = _lane(decay, h)
        c_st[s, h] = dec * c_prev + lax.dot_general(wk.astype(BF16), vb, tn,
                                                    preferred_element_type=F32)
        n_st[s, h] = dec * n_prev + jnp.sum(wk, axis=0, keepdims=True)

        ms = jnp.mean(hcell * hcell, axis=-1, keepdims=True)
        hn = hcell * lax.rsqrt(ms + EPS) * gn_ref[:, sl]
        y = ((hn + skip_ref[:, sl] * xc[:, sl])
             * _sigmoid(op_ref[s, :, sl].astype(F32)))
        yb_ref[s, :, sl] = y.astype(yb_ref.dtype)

    fronts, maths = [], []
    for s in range(nseq):
        fronts.append(front(s))
        maths.append(gate_math(s, fronts[s][4]))
    for h in range(H):
        for s in range(nseq):
            xc, qs, ks, vs, _ = fronts[s]
            head(s, h, xc, qs[h], ks[h], vs[h], *maths[s])


def _mlstm(xm, o_pre, conv_w, conv_b, wq, wk, wv, wif, bif, ml_skip, g_mlnorm,
           *, L=256, nseq=2):
    B, S, D = xm.shape
    H, Dh = ML_HEADS, ML_HEAD_DIM
    row = pl.BlockSpec((nseq, L, D), lambda b, i: (b, i, 0))
    return pl.pallas_call(
        functools.partial(_mlstm_kernel, L=L, nseq=nseq),
        out_shape=jax.ShapeDtypeStruct((B, S, D), BF16),
        grid=(B // nseq, S // L),
        in_specs=[row, row, _resident((CONV_WIDTH, D)), _resident((1, D)),
                  _resident((H, Dh, Dh)), _resident((H, Dh, Dh)), _resident((H, Dh, Dh)),
                  _resident((3, H, 2 * V7X_SUBLANES, Dh)), _resident((2 * V7X_SUBLANES, 1)),
                  _resident((1, D)), _resident((1, D))],
        out_specs=row,
        scratch_shapes=[pltpu.VMEM((nseq, V7X_SUBLANES, D), F32),
                        pltpu.VMEM((nseq, H, Dh, Dh), F32),
                        pltpu.VMEM((nseq, H, 1, Dh), F32),
                        pltpu.VMEM((nseq, 1, V7X_LANES), F32)],
        compiler_params=_params("parallel", "arbitrary"),
        name="mlstm",
    )(xm, o_pre, conv_w, conv_b.reshape(1, D), wq, wk, wv, wif, bif,
      ml_skip.reshape(1, D), g_mlnorm.reshape(1, D))


def _merge_ffn_kernel(h_ref, gt2_ref, ya_ref, yb_ref, ga_ref, gb_ref,
                      wa_ref, wb_ref, wo_ref, sh_ref, sc_ref, gt_ref, g_ref,
                      wg_ref, wu_ref, wd_ref, gfin_ref, o_ref):
    pa = jnp.dot(ya_ref[...], wa_ref[...], preferred_element_type=F32)
    pb = jnp.dot(yb_ref[...], wb_ref[...], preferred_element_type=F32)
    merged = (_sigmoid(ga_ref[...].astype(F32)) * pa
              + _sigmoid(gb_ref[...].astype(F32)) * pb)
    y = jnp.dot(merged.astype(BF16), wo_ref[...], preferred_element_type=F32)
    x = h_ref[...] + gt2_ref[...] * y
    out = _swiglu_residual(x, sh_ref, sc_ref, gt_ref, g_ref, wg_ref, wu_ref, wd_ref)
    ms = jnp.mean(out * out, axis=-1, keepdims=True)
    o_ref[...] = out * lax.rsqrt(ms + EPS) * gfin_ref[...]


def _merge_ffn(h, mods, ya, yb, ga, gb, wa, wb, wo, g, wg, wu, wd, g_final, *, tm=512):
    B, S, D = h.shape
    row = pl.BlockSpec((None, tm, D), lambda b, i: (b, i, 0))
    return pl.pallas_call(
        _merge_ffn_kernel,
        out_shape=jax.ShapeDtypeStruct((B, S, D), F32),
        grid=(B, S // tm),
        in_specs=[row, _mod_spec(5), row, row, row, row,
                  _resident((D, D)), _resident((D, D)), _resident((D, D)),
                  _mod_spec(6), _mod_spec(7), _mod_spec(8)]
        + _ffn_weight_specs() + [_resident((1, D))],
        out_specs=row,
        compiler_params=_params("parallel", "parallel"),
        name="merge_ffn",
    )(h, mods, ya, yb, ga, gb, wa, wb, wo, mods, mods, mods,
      g.reshape(1, D), wg, wu, wd, g_final.reshape(1, D))


def _rope_tables(S):
    inv = ROPE_THETA ** (-jnp.arange(0, DA_HEAD_DIM, 2, dtype=F32) / DA_HEAD_DIM)
    ang = jnp.arange(S, dtype=F32)[:, None] * inv[None, :]
    cos, sin = jnp.cos(ang), jnp.sin(ang)
    cos_g = jnp.concatenate([cos, cos], axis=-1)
    sin_g = jnp.concatenate([-sin, sin], axis=-1)
    reps = V7X_LANES // DA_HEAD_DIM
    return (jnp.tile(cos_g, (1, reps)), jnp.tile(sin_g, (1, reps)), cos_g.T, sin_g.T)


def _gate_weights(w_if, b_if):
    H = ML_HEADS
    pad = V7X_SUBLANES - H
    wt = jnp.swapaxes(w_if, -1, -2)
    wt = wt * jnp.array([1.0, ML_HEAD_DIM ** 0.5, 1.0], F32)[:, None, None, None]
    rows = ((0, 0),) * 2 + ((0, pad), (0, 0))
    wt = jnp.concatenate([jnp.pad(wt[:, :, :H], rows), jnp.pad(wt[:, :, H:], rows)], axis=2)
    bias = jnp.concatenate([jnp.pad(b_if[:H], (0, pad)), jnp.pad(b_if[H:], (0, pad))])
    return wt.astype(BF16), bias.reshape(2 * V7X_SUBLANES, 1)


def kernel(x, c, w_ada, b_ada, g_ff1, w1_gate, w1_up, w1_down, g_mix, w_in,
           lambda_q1, lambda_k1, lambda_q2, lambda_k2, g_subln, conv_w, conv_b,
           w_mq, w_mk, w_mv, w_if, b_if, ml_skip, g_mlnorm, w_proj_a, w_proj_b,
           w_out, g_ff2, w2_gate, w2_up, w2_down, g_final):
    B, S, D = x.shape
    l = 0
    mods = _ada(c, w_ada[l], b_ada[l]).reshape(N_MOD, B, 1, D)
    tables = _rope_tables(S)
    wif, bif = _gate_weights(w_if[l], b_if[l])
    w_in_bf = w_in[l].astype(BF16)
    wqt = w_in_bf[:, :D].T
    wk = w_in_bf[:, D:2 * D]
    wvt = w_in_bf[:, 2 * D:3 * D].T
    wr = w_in_bf[:, 3 * D:]

    h = _ffn(x, mods, 0, g_ff1[l], w1_gate[l].astype(BF16), w1_up[l].astype(BF16),
             w1_down[l].astype(BF16))
    qt, k1, k2, vt, xm, o_pre, ga, gb = _proj(h, mods, g_mix[l], wqt, wk, wvt, wr, tables,
                                              tm=ATTN_BLOCK)
    ya = _attn(qt, k1, k2, vt, lambda_q1[l].reshape(1, -1), lambda_k1[l].reshape(1, -1),
               lambda_q2[l].reshape(1, -1), lambda_k2[l].reshape(1, -1),
               g_subln[l], tb=ATTN_BLOCK)
    yb = _mlstm(xm, o_pre, conv_w[l], conv_b[l], w_mq[l].astype(BF16),
                (w_mk[l] * ML_HEAD_DIM ** -0.5).astype(BF16), w_mv[l].astype(BF16), wif, bif,
                ml_skip[l], g_mlnorm[l])
    return _merge_ffn(h, mods, ya, yb, ga, gb, w_proj_a[l].astype(BF16),
                      w_proj_b[l].astype(BF16), w_out[l].astype(BF16), g_ff2[l],
                      w2_gate[l].astype(BF16), w2_up[l].astype(BF16),
                      w2_down[l].astype(BF16), g_final)
```

```python
import functools
import math

import jax
import jax.numpy as jnp
from jax import lax
from jax.experimental import pallas as pl
from jax.experimental.pallas import tpu as pltpu

D_MODEL = 1024
CHUNK = 64
EPS = 1e-6
DA_HEADS = 8
DA_HEAD_DIM = 64
ROPE_THETA = 10000.0
ML_HEADS = 4
ML_HEAD_DIM = 256
CONV_WIDTH = 4
D_FF = 2816
FFN_RES = 0.5
N_MOD = 9
LAMBDA_INIT = 0.8 - 0.6 * math.exp(-0.3 * 0)

V7X_LANES = 128
V7X_SUBLANES = 8
V7X_MXU_COLS = 256
V7X_VMEM_LIMIT_BYTES = 56 * 1024 * 1024

ATTN_BLOCK = 512
VT_ROWS = 128 + 16

BF16 = jnp.bfloat16
F32 = jnp.float32


def _resident(shape):
    nd = len(shape)
    return pl.BlockSpec(shape, lambda *_: (0,) * nd, pipeline_mode=pl.Buffered(1))


def _params(*sem):
    return pltpu.CompilerParams(dimension_semantics=sem,
                                vmem_limit_bytes=V7X_VMEM_LIMIT_BYTES)


def _mod_spec(k):
    return pl.BlockSpec((None, None, 1, D_MODEL), lambda b, i: (k, b, 0, 0))


def _rms_mod(x, g, shift, scale):
    ms = jnp.mean(x * x, axis=-1, keepdims=True)
    xn = x * lax.rsqrt(ms + EPS) * g
    return xn * (1.0 + scale) + shift


def _sigmoid(x):
    return 1.0 / (1.0 + jnp.exp(-x))


def _ada_kernel(c_ref, w_ref, b_ref, o_ref):
    c = c_ref[...]
    a = c * _sigmoid(c)
    o_ref[...] = jnp.dot(a, w_ref[...], preferred_element_type=F32) + b_ref[...]


def _ada(c, w_ada, b_ada):
    B = c.shape[0]
    return pl.pallas_call(
        _ada_kernel,
        out_shape=jax.ShapeDtypeStruct((N_MOD, B, D_MODEL), F32),
        grid=(N_MOD,),
        in_specs=[
            pl.BlockSpec((B, D_MODEL), lambda j: (0, 0)),
            pl.BlockSpec((D_MODEL, D_MODEL), lambda j: (0, j)),
            pl.BlockSpec((None, 1, D_MODEL), lambda j: (j, 0, 0)),
        ],
        out_specs=pl.BlockSpec((None, B, D_MODEL), lambda j: (j, 0, 0)),
        compiler_params=_params("arbitrary"),
        name="ada",
    )(c, w_ada, b_ada.reshape(N_MOD, 1, D_MODEL))


def _swiglu_residual(x, sh_ref, sc_ref, gt_ref, g_ref, wg_ref, wu_ref, wd_ref):
    u = _rms_mod(x, g_ref[...], sh_ref[...], sc_ref[...]).astype(BF16)
    gate = jnp.dot(u, wg_ref[...], preferred_element_type=F32)
    up = jnp.dot(u, wu_ref[...], preferred_element_type=F32)
    a = (gate * _sigmoid(gate) * up).astype(BF16)
    y = jnp.dot(a, wd_ref[...], preferred_element_type=F32)
    return x + FFN_RES * gt_ref[...] * y


def _ffn_kernel(h_ref, sh_ref, sc_ref, gt_ref, g_ref, wg_ref, wu_ref, wd_ref, o_ref):
    o_ref[...] = _swiglu_residual(h_ref[...], sh_ref, sc_ref, gt_ref, g_ref,
                                  wg_ref, wu_ref, wd_ref)


def _ffn_weight_specs():
    return [_resident((1, D_MODEL)), _resident((D_MODEL, D_FF)),
            _resident((D_MODEL, D_FF)), _resident((D_FF, D_MODEL))]


def _ffn(h, mods, k0, g, wg, wu, wd, *, tm=512):
    B, S, D = h.shape
    row = pl.BlockSpec((None, tm, D), lambda b, i: (b, i, 0))
    return pl.pallas_call(
        _ffn_kernel,
        out_shape=jax.ShapeDtypeStruct((B, S, D), F32),
        grid=(B, S // tm),
        in_specs=[row, _mod_spec(k0), _mod_spec(k0 + 1), _mod_spec(k0 + 2)]
        + _ffn_weight_specs(),
        out_specs=row,
        compiler_params=_params("parallel", "parallel"),
        name="ffn",
    )(h, mods, mods, mods, g.reshape(1, D), wg, wu, wd)


def _rope_rows(t, cos, sin_signed):
    half = DA_HEAD_DIM // 2
    partner = jnp.concatenate([t[half:], t[:half]], axis=0)
    return t * cos + partner * sin_signed


def _rope_lanes(t, cos, sin_signed, first_half):
    outs = []
    for gidx in range(t.shape[1] // V7X_LANES):
        x = t[:, gidx * V7X_LANES:(gidx + 1) * V7X_LANES]
        partner = jnp.where(first_half,
                            pltpu.roll(x, V7X_LANES - DA_HEAD_DIM // 2, 1),
                            pltpu.roll(x, DA_HEAD_DIM // 2, 1))
        outs.append(x * cos + partner * sin_signed)
    return outs


def _proj_kernel(h_ref, sh_ref, sc_ref, g_ref, wqt_ref, wk_ref, wvt_ref, wr_ref,
                 cos_ref, sin_ref, cost_ref, sint_ref,
                 qt_ref, k1_ref, k2_ref, vt_ref, xm_ref, op_ref, ga_ref, gb_ref):
    x = h_ref[...]
    u = _rms_mod(x, g_ref[...], sh_ref[...], sc_ref[...]).astype(BF16)
    nt = (((1,), (1,)), ((), ()))

    q_scale = DA_HEAD_DIM ** -0.5 * math.log2(math.e)
    qt = lax.dot_general(wqt_ref[...], u, nt, preferred_element_type=F32)
    cost = cost_ref[...]
    sint = sint_ref[...]
    for gidx in range(D_MODEL // DA_HEAD_DIM):
        rows = slice(gidx * DA_HEAD_DIM, (gidx + 1) * DA_HEAD_DIM)
        qt_ref[rows, :] = (_rope_rows(qt[rows], cost, sint) * q_scale).astype(BF16)

    lane = lax.broadcasted_iota(jnp.int32, (x.shape[0], V7X_LANES), 1)
    first_half = (lane % DA_HEAD_DIM) < (DA_HEAD_DIM // 2)
    comp0 = lane < DA_HEAD_DIM
    k = jnp.dot(u, wk_ref[...], preferred_element_type=F32)
    for gidx, t in enumerate(_rope_lanes(k, cos_ref[...], sin_ref[...], first_half)):
        sl = slice(gidx * V7X_LANES, (gidx + 1) * V7X_LANES)
        k1_ref[:, sl] = jnp.where(comp0, t, 0.0).astype(BF16)
        k2_ref[:, sl] = jnp.where(comp0, 0.0, t).astype(BF16)

    vt = lax.dot_general(wvt_ref[...], u, nt, preferred_element_type=F32).astype(BF16)
    extra = VT_ROWS - V7X_LANES
    ones_rows = (lax.broadcasted_iota(jnp.int32, (extra, x.shape[0]), 0) == 0).astype(BF16)
    for hd in range(DA_HEADS):
        vt_ref[hd * VT_ROWS:hd * VT_ROWS + V7X_LANES, :] = (
            vt[hd * V7X_LANES:(hd + 1) * V7X_LANES])
        vt_ref[hd * VT_ROWS + V7X_LANES:(hd + 1) * VT_ROWS, :] = ones_rows
    for s, ref in enumerate((xm_ref, op_ref, ga_ref, gb_ref)):
        ref[...] = jnp.dot(u, wr_ref[:, s * D_MODEL:(s + 1) * D_MODEL],
                           preferred_element_type=F32).astype(ref.dtype)


def _proj(h, mods, g, wqt, wk, wvt, wr, tables, *, tm):
    B, S, D = h.shape
    cos, sin, cost, sint = tables
    row = pl.BlockSpec((None, tm, D), lambda b, i: (b, i, 0))
    col = pl.BlockSpec((None, None, D, tm), lambda b, i: (b, i, 0, 0))
    tab = pl.BlockSpec((tm, V7X_LANES), lambda b, i: (i, 0))
    tabt = pl.BlockSpec((DA_HEAD_DIM, tm), lambda b, i: (0, i))
    bf = jax.ShapeDtypeStruct((B, S, D), BF16)
    bft = jax.ShapeDtypeStruct((B, S // tm, D, tm), BF16)
    vrows = DA_HEADS * VT_ROWS
    vspec = pl.BlockSpec((None, None, vrows, tm), lambda b, i: (b, i, 0, 0))
    bfv = jax.ShapeDtypeStruct((B, S // tm, vrows, tm), BF16)
    f32 = jax.ShapeDtypeStruct((B, S, D), F32)
    return pl.pallas_call(
        _proj_kernel,
        out_shape=(bft, bf, bf, bfv, f32, bf, bf, bf),
        grid=(B, S // tm),
        in_specs=[row, _mod_spec(3), _mod_spec(4), _resident((1, D)),
                  _resident((D, D)), _resident((D, D)), _resident((D, D)),
                  _resident((D, 4 * D)), tab, tab, tabt, tabt],
        out_specs=(col, row, row, vspec, row, row, row, row),
        compiler_params=_params("parallel", "parallel"),
        name="mixer_proj",
    )(h, mods, mods, g.reshape(1, D), wqt, wk, wvt, wr, cos, sin, cost, sint)


def _attn_kernel(lq1_ref, lk1_ref, lq2_ref, lk2_ref, gsub_ref,
                 qt_ref, k1_ref, k2_ref, vt_ref, o_ref, s0_ref, s1_ref, s2_ref,
                 m_ref, l_ref, acc_ref, *, tb):
    t = pl.program_id(2)
    wide = 2 * tb
    lam = (jnp.exp(jnp.sum(lq1_ref[...] * lk1_ref[...], axis=-1, keepdims=True))
           - jnp.exp(jnp.sum(lq2_ref[...] * lk2_ref[...], axis=-1, keepdims=True))
           + LAMBDA_INIT)

    def head_ops(hd):
        lanes = slice(hd * V7X_LANES, (hd + 1) * V7X_LANES)
        qt = jnp.concatenate([qt_ref[0, lanes, :], qt_ref[1, lanes, :]],
                             axis=1)
        k_refs = (k1_ref, k2_ref)

        def scores(c, j, s_ref, c0=0):
            start = pl.multiple_of(j * tb, tb)
            s_ref[c, :, c0:] = jnp.dot(k_refs[c][pl.ds(start, tb), lanes], qt[:, c0:],
                                       preferred_element_type=F32)

        def softmax_pv(c, j, s_ref, c0=0, key_off=None):
            vt = vt_ref[j, hd * VT_ROWS:(hd + 1) * VT_ROWS, :]
            s = s_ref[c, :, c0:]
            if key_off is not None:
                shape = (tb, wide - c0)
                kc = (lax.broadcasted_iota(jnp.int32, shape, 0) + key_off) // CHUNK
                qc = (lax.broadcasted_iota(jnp.int32, shape, 1) + c0) // CHUNK
                s = jnp.where(kc <= qc, s, -jnp.inf)
            m = m_ref[hd, c, :, c0:]
            m_new = jnp.maximum(m, jnp.max(s, axis=0, keepdims=True))
            alpha = jnp.exp2(m - m_new)
            p = jnp.exp2(s - m_new)
            pv = jnp.dot(vt, p.astype(BF16), preferred_element_type=F32)
            m_ref[hd, c, :, c0:] = m_new
            l_ref[hd, c, :, c0:] = (alpha * l_ref[hd, c, :, c0:]
                                    + pv[V7X_LANES:V7X_LANES + 1])
            acc_ref[hd, c, :, c0:] = alpha * acc_ref[hd, c, :, c0:] + pv[:V7X_LANES]

        def first_scores(slot):
            m_ref[hd] = jnp.full(m_ref.shape[1:], -jnp.inf, F32)
            l_ref[hd] = jnp.zeros(l_ref.shape[1:], F32)
            acc_ref[hd] = jnp.zeros(acc_ref.shape[1:], F32)
            for c in range(2):
                scores(c, 0, slot)

        def block_pair(u, slot_a, slot_b):
            for cur, nxt, j in ((slot_a, slot_b, 2 * u), (slot_b, slot_a, 2 * u + 1)):
                for c in range(2):
                    scores(c, j + 1, nxt)
                    softmax_pv(c, j, cur)

        def full_blocks(slot_a, slot_b, pairs):
            for u in range(pairs):
                block_pair(u, slot_a, slot_b)

        def diagonal(slot_a, slot_b, between=None):
            for c in range(2):
                scores(c, 2 * t + 1, slot_b, c0=tb)
                softmax_pv(c, 2 * t, slot_a, key_off=0)
            if between is not None:
                between()
            for c in range(2):
                softmax_pv(c, 2 * t + 1, slot_b, c0=tb, key_off=tb)
            o = (acc_ref[hd, 0] * (1.0 / l_ref[hd, 0])
                 - acc_ref[hd, 1] * (lam / l_ref[hd, 1]))
            ms = jnp.mean(o * o, axis=0, keepdims=True)
            o = o * lax.rsqrt(ms + EPS) * (gsub_ref[...] * (1.0 - LAMBDA_INIT))
            o_ref[:, lanes] = o.T.astype(o_ref.dtype)

        return first_scores, full_blocks, diagonal

    first0, full0, diag0 = head_ops(0)
    first1, full1, diag1 = head_ops(1)
    @pl.when(t == 0)
    def _():
        first0(s0_ref)
        diag0(s0_ref, s1_ref, between=lambda: first1(s2_ref))
        diag1(s2_ref, s1_ref)

    def with_full_blocks(pairs):
        first0(s0_ref)
        full0(s0_ref, s1_ref, pairs)
        diag0(s0_ref, s1_ref, between=lambda: first1(s2_ref))
        full1(s2_ref, s1_ref, pairs)
        diag1(s2_ref, s1_ref)

    for pairs in range(1, vt_ref.shape[0] // 2):
        pl.when(t == pairs)(functools.partial(with_full_blocks, pairs))


def _attn(qt, k1, k2, vt, lq1, lk1, lq2, lk2, g_subln, *, tb):
    B, S, D = k1.shape
    nb = S // tb
    hp = 2
    lam_spec = pl.BlockSpec((1, DA_HEAD_DIM), lambda b, h, i: (0, 0))
    kspec = pl.BlockSpec((None, S, hp * V7X_LANES), lambda b, h, i: (b, 0, h))
    steps = nb // 2
    stat = pltpu.VMEM((hp, 2, 1, 2 * tb), F32)
    return pl.pallas_call(
        functools.partial(_attn_kernel, tb=tb),
        out_shape=jax.ShapeDtypeStruct((B, S, D), BF16),
        grid=(B, DA_HEADS // hp, steps),
        in_specs=[lam_spec] * 4 + [
            pl.BlockSpec((V7X_LANES, 1), lambda b, h, i: (0, 0)),
            pl.BlockSpec((None, 2, hp * V7X_LANES, tb), lambda b, h, i: (b, i, h, 0)),
            kspec, kspec,
            pl.BlockSpec((None, nb, hp * VT_ROWS, tb), lambda b, h, i: (b, 0, h, 0))],
        out_specs=pl.BlockSpec((None, 2 * tb, hp * V7X_LANES), lambda b, h, i: (b, i, h)),
        scratch_shapes=[pltpu.VMEM((2, tb, 2 * tb), F32)] * 3 + [
            stat, stat, pltpu.VMEM((hp, 2, V7X_LANES, 2 * tb), F32)],
        compiler_params=_params("parallel", "parallel", "arbitrary"),
        name="diff_attn",
    )(lq1, lk1, lq2, lk2, g_subln.reshape(V7X_LANES, 1), qt, k1, k2, vt)


def _scan_rows(x, op, identity):
    n = x.shape[0]
    row = lax.broadcasted_iota(jnp.int32, x.shape, 0)
    shift = 1
    while shift < n:
        x = op(x, jnp.where(row >= shift, pltpu.roll(x, shift, 0), identity))
        shift *= 2
    return x


def _lane(x, h):
    return x[:, h:h + 1]


def _mlstm_kernel(xm_ref, op_ref, cw_ref, cb_ref, wq_ref, wk_ref, wv_ref,
                  wif_ref, bif_ref, skip_ref, gn_ref, yb_ref,
                  xbuf, c_st, n_st, m_st, *, L, nseq):
    H, Dh = ML_HEADS, ML_HEAD_DIM
    halo = V7X_SUBLANES

    @pl.when(pl.program_id(1) == 0)
    def _():
        xbuf[...] = jnp.zeros_like(xbuf)
        c_st[...] = jnp.zeros_like(c_st)
        n_st[...] = jnp.zeros_like(n_st)
        m_st[...] = jnp.zeros_like(m_st)

    def front(s):
        xm = xm_ref[s]
        xext = jnp.concatenate([xbuf[s], xm], axis=0)
        conv = cb_ref[...] + cw_ref[CONV_WIDTH - 1:CONV_WIDTH, :] * xm
        for d in range(1, CONV_WIDTH):
            conv = conv + (cw_ref[CONV_WIDTH - 1 - d:CONV_WIDTH - d, :]
                           * pltpu.roll(xext, d, 0)[halo:, :])
        xbuf[s] = xm[L - halo:L, :]
        xc = conv * _sigmoid(conv)
        qs, ks, vs = [], [], []
        gates = jnp.broadcast_to(bif_ref[...], (2 * V7X_SUBLANES, L))
        for h in range(H):
            sl = slice(h * Dh, (h + 1) * Dh)
            xc_h = xc[:, sl].astype(BF16)
            q = jnp.dot(xc_h, wq_ref[h], preferred_element_type=F32)
            k = jnp.dot(xc_h, wk_ref[h], preferred_element_type=F32)
            v = jnp.dot(xm[:, sl].astype(BF16), wv_ref[h], preferred_element_type=F32)
            for j, t in enumerate((q, k, v)):
                gates = gates + lax.dot_general(wif_ref[j, h], t.astype(BF16), nt,
                                                preferred_element_type=F32)
            qs.append(q)
            ks.append(k)
            vs.append(v)
        return xc, qs, ks, vs, gates

    def gate_math(s, gates):
        pad = jnp.zeros((V7X_LANES - V7X_SUBLANES, L), F32)

        def columns(rows):
            return jnp.concatenate([rows, pad], axis=0).T

        gi = columns(gates[:V7X_SUBLANES])
        b = _scan_rows(columns(jax.nn.log_sigmoid(gates[V7X_SUBLANES:])), jnp.add, 0.0)
        a = gi - b
        m_prev = m_st[s]
        mx = jnp.maximum(m_prev, _scan_rows(a, jnp.maximum, -jnp.inf))
        b_last = b[L - 1:L, :]
        m_new = b_last + mx[L - 1:L, :]
        decay = jnp.exp(b_last + m_prev - m_new)
        wg = jnp.exp(b_last + a - m_new)
        m_st[s] = m_new
        return (-mx, jnp.exp(m_prev - mx), wg, jnp.exp(-(b + mx))), a.T, decay

    tril = (lax.broadcasted_iota(jnp.int32, (L, L), 1)
            <= lax.broadcasted_iota(jnp.int32, (L, L), 0))
    nt = (((1,), (1,)), ((), ()))
    tn = (((0,), (0,)), ((), ()))

    def head(s, h, xc, q, k, v, cols, a, decay):
        neg_mx, w_carry, w_state, exp_neg_m = cols
        sl = slice(h * Dh, (h + 1) * Dh)
        qb, kb, vb = q.astype(BF16), k.astype(BF16), v.astype(BF16)
        w = jnp.exp(jnp.where(tril, _lane(neg_mx, h) + a[h:h + 1, :], -jnp.inf))
        sc = lax.dot_general(qb, kb, nt, preferred_element_type=F32) * w
        w_inter = _lane(w_carry, h)
        c_prev = c_st[s, h]
        n_prev = n_st[s, h]
        num = (jnp.dot(sc.astype(BF16), vb, preferred_element_type=F32)
               + w_inter * jnp.dot(qb, c_prev.astype(BF16), preferred_element_type=F32))
        den = (jnp.sum(sc, axis=-1, keepdims=True)
               + w_inter * jnp.sum(q * n_prev, axis=-1, keepdims=True))
        hcell = num / jnp.maximum(jnp.abs(den), _lane(exp_neg_m, h))

        wk = _lane(w_state, h) * k
        dec = _lane(decay, h)
        c_st[s, h] = dec * c_prev + lax.dot_general(wk.astype(BF16), vb, tn,
                                                    preferred_element_type=F32)
        n_st[s, h] = dec * n_prev + jnp.sum(wk, axis=0, keepdims=True)

        ms = jnp.mean(hcell * hcell, axis=-1, keepdims=True)
        hn = hcell * lax.rsqrt(ms + EPS) * gn_ref[:, sl]
        y = ((hn + skip_ref[:, sl] * xc[:, sl])
             * _sigmoid(op_ref[s, :, sl].astype(F32)))
        yb_ref[s, :, sl] = y.astype(yb_ref.dtype)

    fronts, maths = [], []
    for s in range(nseq):
        fronts.append(front(s))
        maths.append(gate_math(s, fronts[s][4]))
    for h in range(H):
        for s in range(nseq):
            xc, qs, ks, vs, _ = fronts[s]
            head(s, h, xc, qs[h], ks[h], vs[h], *maths[s])


def _mlstm(xm, o_pre, conv_w, conv_b, wq, wk, wv, wif, bif, ml_skip, g_mlnorm,
           *, L=256, nseq=2):
    B, S, D = xm.shape
    H, Dh = ML_HEADS, ML_HEAD_DIM
    row = pl.BlockSpec((nseq, L, D), lambda b, i: (b, i, 0))
    return pl.pallas_call(
        functools.partial(_mlstm_kernel, L=L, nseq=nseq),
        out_shape=jax.ShapeDtypeStruct((B, S, D), BF16),
        grid=(B // nseq, S // L),
        in_specs=[row, row, _resident((CONV_WIDTH, D)), _resident((1, D)),
                  _resident((H, Dh, Dh)), _resident((H, Dh, Dh)), _resident((H, Dh, Dh)),
                  _resident((3, H, 2 * V7X_SUBLANES, Dh)), _resident((2 * V7X_SUBLANES, 1)),
                  _resident((1, D)), _resident((1, D))],
        out_specs=row,
        scratch_shapes=[pltpu.VMEM((nseq, V7X_SUBLANES, D), F32),
                        pltpu.VMEM((nseq, H, Dh, Dh), F32),
                        pltpu.VMEM((nseq, H, 1, Dh), F32),
                        pltpu.VMEM((nseq, 1, V7X_LANES), F32)],
        compiler_params=_params("parallel", "arbitrary"),
        name="mlstm",
    )(xm, o_pre, conv_w, conv_b.reshape(1, D), wq, wk, wv, wif, bif,
      ml_skip.reshape(1, D), g_mlnorm.reshape(1, D))


def _merge_ffn_kernel(h_ref, gt2_ref, ya_ref, yb_ref, ga_ref, gb_ref,
                      wa_ref, wb_ref, wo_ref, sh_ref, sc_ref, gt_ref, g_ref,
                      wg_ref, wu_ref, wd_ref, gfin_ref, o_ref):
    pa = jnp.dot(ya_ref[...], wa_ref[...], preferred_element_type=F32)
    pb = jnp.dot(yb_ref[...], wb_ref[...], preferred_element_type=F32)
    merged = (_sigmoid(ga_ref[...].astype(F32)) * pa
              + _sigmoid(gb_ref[...].astype(F32)) * pb)
    y = jnp.dot(merged.astype(BF16), wo_ref[...], preferred_element_type=F32)
    x = h_ref[...] + gt2_ref[...] * y
    out = _swiglu_residual(x, sh_ref, sc_ref, gt_ref, g_ref, wg_ref, wu_ref, wd_ref)
    ms = jnp.mean(out * out, axis=-1, keepdims=True)
    o_ref[...] = out * lax.rsqrt(ms + EPS) * gfin_ref[...]


def _merge_ffn(h, mods, ya, yb, ga, gb, wa, wb, wo, g, wg, wu, wd, g_final, *, tm=512):
    B, S, D = h.shape
    row = pl.BlockSpec((None, tm, D), lambda b, i: (b, i, 0))
    return pl.pallas_call(
        _merge_ffn_kernel,
        out_shape=jax.ShapeDtypeStruct((B, S, D), F32),
        grid=(B, S // tm),
        in_specs=[row, _mod_spec(5), row, row, row, row,
                  _resident((D, D)), _resident((D, D)), _resident((D, D)),
                  _mod_spec(6), _mod_spec(7), _mod_spec(8)]
        + _ffn_weight_specs() + [_resident((1, D))],
        out_specs=row,
        compiler_params=_params("parallel", "parallel"),
        name="merge_ffn",
    )(h, mods, ya, yb, ga, gb, wa, wb, wo, mods, mods, mods,
      g.reshape(1, D), wg, wu, wd, g_final.reshape(1, D))


def _rope_tables(S):
    inv = ROPE_THETA ** (-jnp.arange(0, DA_HEAD_DIM, 2, dtype=F32) / DA_HEAD_DIM)
    ang = jnp.arange(S, dtype=F32)[:, None] * inv[None, :]
    cos, sin = jnp.cos(ang), jnp.sin(ang)
    cos_g = jnp.concatenate([cos, cos], axis=-1)
    sin_g = jnp.concatenate([-sin, sin], axis=-1)
    reps = V7X_LANES // DA_HEAD_DIM
    return (jnp.tile(cos_g, (1, reps)), jnp.tile(sin_g, (1, reps)), cos_g.T, sin_g.T)


def _gate_weights(w_if, b_if):
    H = ML_HEADS
    pad = V7X_SUBLANES - H
    wt = jnp.swapaxes(w_if, -1, -2)
    wt = wt * jnp.array([1.0, ML_HEAD_DIM ** 0.5, 1.0], F32)[:, None, None, None]
    rows = ((0, 0),) * 2 + ((0, pad), (0, 0))
    wt = jnp.concatenate([jnp.pad(wt[:, :, :H], rows), jnp.pad(wt[:, :, H:], rows)], axis=2)
    bias = jnp.concatenate([jnp.pad(b_if[:H], (0, pad)), jnp.pad(b_if[H:], (0, pad))])
    return wt.astype(BF16), bias.reshape(2 * V7X_SUBLANES, 1)


def kernel(x, c, w_ada, b_ada, g_ff1, w1_gate, w1_up, w1_down, g_mix, w_in,
           lambda_q1, lambda_k1, lambda_q2, lambda_k2, g_subln, conv_w, conv_b,
           w_mq, w_mk, w_mv, w_if, b_if, ml_skip, g_mlnorm, w_proj_a, w_proj_b,
           w_out, g_ff2, w2_gate, w2_up, w2_down, g_final):
    B, S, D = x.shape
    l = 0
    mods = _ada(c, w_ada[l], b_ada[l]).reshape(N_MOD, B, 1, D)
    tables = _rope_tables(S)
    wif, bif = _gate_weights(w_if[l], b_if[l])
    w_in_bf = w_in[l].astype(BF16)
    wqt = w_in_bf[:, :D].T
    wk = w_in_bf[:, D:2 * D]
    wvt = w_in_bf[:, 2 * D:3 * D].T
    wr = w_in_bf[:, 3 * D:]

    h = _ffn(x, mods, 0, g_ff1[l], w1_gate[l].astype(BF16), w1_up[l].astype(BF16),
             w1_down[l].astype(BF16))
    qt, k1, k2, vt, xm, o_pre, ga, gb = _proj(h, mods, g_mix[l], wqt, wk, wvt, wr, tables,
                                              tm=ATTN_BLOCK)
    ya = _attn(qt, k1, k2, vt, lambda_q1[l].reshape(1, -1), lambda_k1[l].reshape(1, -1),
               lambda_q2[l].reshape(1, -1), lambda_k2[l].reshape(1, -1),
               g_subln[l], tb=ATTN_BLOCK)
    yb = _mlstm(xm, o_pre, conv_w[l], conv_b[l], w_mq[l].astype(BF16),
                (w_mk[l] * ML_HEAD_DIM ** -0.5).astype(BF16), w_mv[l].astype(BF16), wif, bif,
                ml_skip[l], g_mlnorm[l])
    return _merge_ffn(h, mods, ya, yb, ga, gb, w_proj_a[l].astype(BF16),
                      w_proj_b[l].astype(BF16), w_out[l].astype(BF16), g_ff2[l],
                      w2_gate[l].astype(BF16), w2_up[l].astype(BF16),
                      w2_down[l].astype(BF16), g_final)
```
